```python
import math
import jax
import jax.numpy as jnp
from jax import lax
import numpy as np

D_MODEL = 2048
BATCH = 1
SEQ = 8192
DEPTH = 2
DEC_BATCH = 128
DEC_SEQ = 4
PAST_LEN = 2048
PAGE_SIZE = 128

D_MIX = D_MODEL
SB_HEAD_DIM = 128
SB_DIM = D_MIX // 2
SB_HEADS = SB_DIM // SB_HEAD_DIM
SB_BIAS_INIT = -5.0
CONV_DIM = D_MIX - SB_DIM
CONV_WIDTH = 31
N_MEM = 256
X_HEADS = 4
X_HEAD_DIM = D_MODEL // X_HEADS
D_FF = -(-8 * D_MODEL // (3 * 256)) * 256
D_IN = 3 * SB_DIM + 2 * CONV_DIM
Q_BLOCK = 128
EPS = 1e-6

kernel_name = 'stickbreak_conformer_hybrid_step'


def rmsnorm(x, g):
    xf = x.astype(jnp.float32)
    y = xf * lax.rsqrt(jnp.mean(xf * xf, axis=-1, keepdims=True) + EPS)
    return (y * g.astype(jnp.float32)).astype(x.dtype)


def layernorm(x, g, b):
    xf = x.astype(jnp.float32)
    mu = jnp.mean(xf, axis=-1, keepdims=True)
    xc = xf - mu
    var = jnp.mean(xc * xc, axis=-1, keepdims=True)
    y = xc * lax.rsqrt(var + EPS) * g.astype(jnp.float32) + b.astype(jnp.float32)
    return y.astype(x.dtype)


def in_proj(h, w):
    B, T, _ = h.shape
    p = h @ w
    q, k, v, a, gt = jnp.split(p, [SB_DIM, 2 * SB_DIM, 3 * SB_DIM, 3 * SB_DIM + CONV_DIM], axis=-1)
    hs = (B, T, SB_HEADS, SB_HEAD_DIM)
    return q.reshape(hs), k.reshape(hs), v.reshape(hs), a, gt


def sb_weights(z, mask):
    log_keep = jnp.where(mask, jax.nn.log_sigmoid(-z), 0.0)
    later = jnp.flip(jnp.cumsum(jnp.flip(log_keep, -1), -1), -1) - log_keep
    return jnp.where(mask, jnp.exp(jax.nn.log_sigmoid(z) + later), 0.0)


def sb_prompt(q, k, v, b_sb):
    B, S, H, Dh = q.shape
    nb = S // Q_BLOCK
    qb = q.reshape(B, nb, Q_BLOCK, H, Dh).transpose(1, 0, 2, 3, 4)
    k_pos = jnp.arange(S)
    scale = Dh ** -0.5
    bias = b_sb.astype(jnp.float32)[None, :, None, None]

    def block(args):
        qi, i = args
        q_pos = i * Q_BLOCK + jnp.arange(Q_BLOCK)
        z = jnp.einsum('bqhd,bkhd->bhqk', qi, k, preferred_element_type=jnp.float32) * scale + bias
        w = sb_weights(z, k_pos[None, :] < q_pos[:, None])
        o = jnp.einsum('bhqk,bkhd->bqhd', w.astype(v.dtype), v, preferred_element_type=jnp.float32)
        return o.astype(q.dtype)

    o = lax.map(block, (qb, jnp.arange(nb)))
    return o.transpose(1, 0, 2, 3, 4).reshape(B, S, H, Dh)


def sb_sample(q, k_new, v_new, k_past, v_past, b_sb):
    T = q.shape[1]
    P = k_past.shape[1]
    scale = q.shape[-1] ** -0.5
    bias = b_sb.astype(jnp.float32)[None, :, None, None]
    z = jnp.concatenate([
        jnp.einsum('bqhd,bkhd->bhqk', q, k_past, preferred_element_type=jnp.float32),
        jnp.einsum('bqhd,bkhd->bhqk', q, k_new, preferred_element_type=jnp.float32)], axis=-1) * scale + bias
    q_pos = P + jnp.arange(T)
    k_pos = jnp.arange(P + T)
    w = sb_weights(z, k_pos[None, :] < q_pos[:, None]).astype(v_new.dtype)
    o = (jnp.einsum('bhqk,bkhd->bqhd', w[..., :P], v_past, preferred_element_type=jnp.float32)
         + jnp.einsum('bhqk,bkhd->bqhd', w[..., P:], v_new, preferred_element_type=jnp.float32))
    return o.astype(q.dtype)


def conv_module(a, gt, hist, w_dw, b_dw, g_ln, b_ln):
    u = a * jax.nn.sigmoid(gt)
    u_full = jnp.concatenate([hist.astype(u.dtype), u], axis=1)
    C = u.shape[-1]
    c = lax.conv_general_dilated(u_full, w_dw[:, None, :].astype(u.dtype), (1,), 'VALID',
                                 dimension_numbers=('NWC', 'WIO', 'NWC'),
                                 feature_group_count=C) + b_dw
    c = jax.nn.silu(layernorm(c, g_ln, b_ln))
    return c, u_full[:, -(CONV_WIDTH - 1):]


def mem_kv(mem, g_mem, w_k, w_v):
    B, M, _ = mem.shape
    m = rmsnorm(mem, g_mem)
    return (m @ w_k).reshape(B, M, X_HEADS, X_HEAD_DIM), (m @ w_v).reshape(B, M, X_HEADS, X_HEAD_DIM)


def cross_attn(h, mk, mv, w_q, w_o):
    B, T, _ = h.shape
    q = (h @ w_q).reshape(B, T, X_HEADS, X_HEAD_DIM)
    s = jnp.einsum('bqhd,bmhd->bhqm', q, mk, preferred_element_type=jnp.float32) * (X_HEAD_DIM ** -0.5)
    p = jax.nn.softmax(s, axis=-1).astype(mv.dtype)
    o = jnp.einsum('bhqm,bmhd->bqhd', p, mv, preferred_element_type=jnp.float32).astype(h.dtype)
    return o.reshape(B, T, D_MODEL) @ w_o


def swiglu(h, w_gate, w_up, w_down):
    return (jax.nn.silu(h @ w_gate) * (h @ w_up)) @ w_down


def setup_inputs(seed: int = 0) -> dict:
    key = jax.random.key(seed)
    ks = jax.random.split(key, 32)
    n_pages = PAST_LEN // PAGE_SIZE
    n_used = DEC_BATCH * n_pages
    n_pool = n_used + max(1, n_used // 4)
    f32 = jnp.float32

    def nrm(k, shape, scale=1.0):
        return jax.random.normal(k, shape, f32) * scale

    def gain(k, shape):
        return 1.0 + 0.01 * jax.random.normal(k, shape, f32)

    page_table = jax.random.permutation(ks[0], n_pool)[:n_used].reshape(DEC_BATCH, n_pages).astype(jnp.int32)
    return {
        'x_prompt': nrm(ks[1], (BATCH, SEQ, D_MODEL)),
        'x_sample': nrm(ks[2], (DEC_BATCH, DEC_SEQ, D_MODEL)),
        'cache_sb_k': nrm(ks[3], (DEPTH, n_pool, PAGE_SIZE, SB_HEADS, SB_HEAD_DIM)),
        'cache_sb_v': nrm(ks[4], (DEPTH, n_pool, PAGE_SIZE, SB_HEADS, SB_HEAD_DIM)),
        'state_conv': nrm(ks[5], (DEPTH, DEC_BATCH, CONV_WIDTH - 1, CONV_DIM), 0.5),
        'cache_mem_k': nrm(ks[6], (DEPTH, DEC_BATCH, N_MEM, X_HEADS, X_HEAD_DIM)),
        'cache_mem_v': nrm(ks[7], (DEPTH, DEC_BATCH, N_MEM, X_HEADS, X_HEAD_DIM)),
        'page_table': page_table,
        'mem_prompt': nrm(ks[8], (BATCH, N_MEM, D_MODEL)),
        'g_mix': gain(ks[9], (DEPTH, D_MODEL)),
        'w_in': nrm(ks[10], (DEPTH, D_MODEL, D_IN), D_MODEL ** -0.5),
        'b_sb': SB_BIAS_INIT + nrm(ks[27], (DEPTH, SB_HEADS), 0.1),
        'w_dw': nrm(ks[11], (DEPTH, CONV_WIDTH, CONV_DIM), CONV_WIDTH ** -0.5),
        'b_dw': nrm(ks[12], (DEPTH, CONV_DIM), 0.01),
        'g_cln': gain(ks[13], (DEPTH, CONV_DIM)),
        'b_cln': nrm(ks[14], (DEPTH, CONV_DIM), 0.01),
        'w_out': nrm(ks[15], (DEPTH, D_MIX, D_MODEL), D_MIX ** -0.5),
        'g_cross': gain(ks[16], (DEPTH, D_MODEL)),
        'g_mem': gain(ks[17], (DEPTH, D_MODEL)),
        'w_xq': nrm(ks[18], (DEPTH, D_MODEL, D_MODEL), D_MODEL ** -0.5),
        'w_xk': nrm(ks[19], (DEPTH, D_MODEL, D_MODEL), D_MODEL ** -0.5),
        'w_xv': nrm(ks[20], (DEPTH, D_MODEL, D_MODEL), D_MODEL ** -0.5),
        'w_xo': nrm(ks[21], (DEPTH, D_MODEL, D_MODEL), D_MODEL ** -0.5),
        'g_ffn': gain(ks[22], (DEPTH, D_MODEL)),
        'w_gate': nrm(ks[23], (DEPTH, D_MODEL, D_FF), D_MODEL ** -0.5),
        'w_up': nrm(ks[24], (DEPTH, D_MODEL, D_FF), D_MODEL ** -0.5),
        'w_down': nrm(ks[25], (DEPTH, D_FF, D_MODEL), D_FF ** -0.5),
        'g_final': gain(ks[26], (D_MODEL,)),
    }


def reference(x_prompt, x_sample, cache_sb_k, cache_sb_v, state_conv, cache_mem_k, cache_mem_v,
              page_table, mem_prompt, g_mix, w_in, b_sb, w_dw, b_dw, g_cln, b_cln, w_out,
              g_cross, g_mem, w_xq, w_xk, w_xv, w_xo, g_ffn, w_gate, w_up, w_down, g_final):
    B, S, _ = x_prompt.shape
    DB, T, _ = x_sample.shape
    n_pages = page_table.shape[1]
    past = n_pages * cache_sb_k.shape[2]
    zero_hist = jnp.zeros((B, CONV_WIDTH - 1, CONV_DIM), x_prompt.dtype)

    xp, xs = x_prompt, x_sample
    skp, svp, sks, svs, cvp, cvs, mkp, mvp = [], [], [], [], [], [], [], []
    for l in range(DEPTH):
        h = rmsnorm(xp, g_mix[l])
        q, k, v, a, gt = in_proj(h, w_in[l])
        o_sb = sb_prompt(q, k, v, b_sb[l])
        c, hist_p = conv_module(a, gt, zero_hist, w_dw[l], b_dw[l], g_cln[l], b_cln[l])
        xp = xp + jnp.concatenate([o_sb.reshape(B, S, SB_DIM), c], axis=-1) @ w_out[l]
        skp.append(k)
        svp.append(v)
        cvp.append(hist_p)

        h = rmsnorm(xs, g_mix[l])
        q, k, v, a, gt = in_proj(h, w_in[l])
        k_past = cache_sb_k[l][page_table].reshape(DB, past, SB_HEADS, SB_HEAD_DIM)
        v_past = cache_sb_v[l][page_table].reshape(DB, past, SB_HEADS, SB_HEAD_DIM)
        o_sb = sb_sample(q, k, v, k_past, v_past, b_sb[l])
        c, hist_s = conv_module(a, gt, state_conv[l], w_dw[l], b_dw[l], g_cln[l], b_cln[l])
        xs = xs + jnp.concatenate([o_sb.reshape(DB, T, SB_DIM), c], axis=-1) @ w_out[l]
        sks.append(k)
        svs.append(v)
        cvs.append(hist_s)

        mk, mv = mem_kv(mem_prompt, g_mem[l], w_xk[l], w_xv[l])
        mkp.append(mk)
        mvp.append(mv)
        xp = xp + cross_attn(rmsnorm(xp, g_cross[l]), mk, mv, w_xq[l], w_xo[l])
        xs = xs + cross_attn(rmsnorm(xs, g_cross[l]), cache_mem_k[l], cache_mem_v[l], w_xq[l], w_xo[l])

        xp = xp + swiglu(rmsnorm(xp, g_ffn[l]), w_gate[l], w_up[l], w_down[l])
        xs = xs + swiglu(rmsnorm(xs, g_ffn[l]), w_gate[l], w_up[l], w_down[l])

    y_prompt = rmsnorm(xp, g_final)
    y_sample = rmsnorm(xs, g_final)
    return (y_prompt, y_sample, jnp.stack(skp), jnp.stack(svp), jnp.stack(sks), jnp.stack(svs),
            jnp.stack(cvp), jnp.stack(cvs), jnp.stack(mkp), jnp.stack(mvp))
```

```python
import functools

import jax
import jax.numpy as jnp
from jax import lax
from jax.experimental import pallas as pl
from jax.experimental.pallas import tpu as pltpu

F32 = jnp.float32
BF16 = jnp.bfloat16
EPS = 1e-6

V7X_VMEM_LIMIT_BYTES = 56 * 1024 * 1024
SB_HEAD_DIM = 128
SB_KEY_CHUNK = 256
PROMPT_Q_BLOCK = 512
CONV_ROW_CHUNK = 32


def _params(*semantics):
    return pltpu.CompilerParams(dimension_semantics=semantics, vmem_limit_bytes=V7X_VMEM_LIMIT_BYTES)


def _rms_bf16(x, g):
    y = x * lax.rsqrt(jnp.mean(x * x, axis=-1, keepdims=True) + EPS)
    return (y * g).astype(BF16)


def _sigmoid(x):
    return 1.0 / (1.0 + jnp.exp(-x))


def _softplus(z):
    return jnp.maximum(z, 0.0) + jnp.log(1.0 + jnp.exp(-jnp.abs(z)))


def _dot(a, b):
    return jnp.dot(a, b, preferred_element_type=F32)


def _dot_nt(a, b):
    return lax.dot_general(a, b, (((1,), (1,)), ((), ())), preferred_element_type=F32)


def _norm_matmul_kernel(x_ref, g_ref, w_ref, o_ref, h_ref):
    @pl.when(pl.program_id(1) == 0)
    def _():
        h_ref[...] = _rms_bf16(x_ref[...], g_ref[...])

    o_ref[...] = _dot(h_ref[...], w_ref[...].astype(BF16)).astype(o_ref.dtype)


def norm_matmul(x, g3, w3, layer, *, tm, tn):
    m, k = x.shape
    n = w3.shape[-1]
    return pl.pallas_call(
        _norm_matmul_kernel,
        grid=(m // tm, n // tn),
        in_specs=[
            pl.BlockSpec((tm, k), lambda i, j: (i, 0)),
            pl.BlockSpec((None, 1, k), lambda i, j: (layer, 0, 0)),
            pl.BlockSpec((None, k, tn), lambda i, j: (layer, 0, j)),
        ],
        out_specs=pl.BlockSpec((tm, tn), lambda i, j: (i, j)),
        out_shape=jax.ShapeDtypeStruct((m, n), F32),
        scratch_shapes=[pltpu.VMEM((tm, k), BF16)],
        compiler_params=_params("arbitrary", "arbitrary"),
        name="norm_matmul",
    )(x, g3, w3)


def _matmul_residual_kernel(n_pieces, *refs):
    a_refs = refs[:n_pieces]
    w_refs = refs[n_pieces:2 * n_pieces]
    r_ref, o_ref = refs[2 * n_pieces:]
    acc = r_ref[...]
    for a_ref, w_ref in zip(a_refs, w_refs):
        acc = acc + _dot(a_ref[...].astype(BF16), w_ref[...].astype(BF16))
    o_ref[...] = acc


def matmul_residual(pieces, w3, layer, res, *, tm, tn):
    m = res.shape[0]
    n = w3.shape[-1]
    kp = pieces[0].shape[-1]
    n_pieces = len(pieces)
    in_specs = [pl.BlockSpec((tm, kp), lambda i, j: (i, 0)) for _ in pieces]
    in_specs += [pl.BlockSpec((None, kp, tn), functools.partial(lambda p, i, j: (layer, p, j), p))
                 for p in range(n_pieces)]
    in_specs += [pl.BlockSpec((tm, tn), lambda i, j: (i, j))]
    return pl.pallas_call(
        functools.partial(_matmul_residual_kernel, n_pieces),
        grid=(m // tm, n // tn),
        in_specs=in_specs,
        out_specs=pl.BlockSpec((tm, tn), lambda i, j: (i, j)),
        out_shape=jax.ShapeDtypeStruct((m, n), F32),
        compiler_params=_params("arbitrary", "arbitrary"),
        name="matmul_residual",
    )(*pieces, *([w3] * n_pieces), res)


def _ffn_kernel(x_ref, g_ref, wg_ref, wu_ref, wd_ref, o_ref, h_ref):
    @pl.when(pl.program_id(1) == 0)
    def _():
        x = x_ref[...]
        h_ref[...] = _rms_bf16(x, g_ref[...])
        o_ref[...] = x

    h = h_ref[...]
    gate = _dot(h, wg_ref[...].astype(BF16))
    up = _dot(h, wu_ref[...].astype(BF16))
    act = (gate * _sigmoid(gate) * up).astype(BF16)
    o_ref[...] += _dot(act, wd_ref[...].astype(BF16))


def ffn_residual(x, g3, wg3, wu3, wd3, layer, *, tm, tf):
    m, d = x.shape
    f = wg3.shape[-1]
    return pl.pallas_call(
        _ffn_kernel,
        grid=(m // tm, f // tf),
        in_specs=[
            pl.BlockSpec((tm, d), lambda i, j: (i, 0), pipeline_mode=pl.Buffered(1)),
            pl.BlockSpec((None, 1, d), lambda i, j: (layer, 0, 0)),
            pl.BlockSpec((None, d, tf), lambda i, j: (layer, 0, j)),
            pl.BlockSpec((None, d, tf), lambda i, j: (layer, 0, j)),
            pl.BlockSpec((None, tf, d), lambda i, j: (layer, j, 0)),
        ],
        out_specs=pl.BlockSpec((tm, d), lambda i, j: (i, 0)),
        out_shape=jax.ShapeDtypeStruct((m, d), F32),
        scratch_shapes=[pltpu.VMEM((tm, d), BF16)],
        compiler_params=_params("arbitrary", "arbitrary"),
        name="ffn_residual",
    )(x, g3, wg3, wu3, wd3)


def _cross_shared_kernel(scale, x_ref, g_ref, wq_ref, mk_ref, mv_ref, wo_ref, o_ref, h_ref):
    @pl.when(pl.program_id(1) == 0)
    def _():
        x = x_ref[...]
        h_ref[...] = _rms_bf16(x, g_ref[...])
        o_ref[...] = x

    q = _dot(h_ref[...], wq_ref[...].astype(BF16))
    s = _dot_nt(q.astype(BF16), mk_ref[...].astype(BF16)) * scale
    e = jnp.exp(s - jnp.max(s, axis=-1, keepdims=True))
    p = e / jnp.sum(e, axis=-1, keepdims=True)
    o = _dot(p.astype(BF16), mv_ref[...].astype(BF16))
    o_ref[...] += _dot(o.astype(BF16), wo_ref[...].astype(BF16))


def cross_shared_residual(x, g3, wq3, mk, mv, wo3, layer, n_heads, *, tm):
    m, d = x.shape
    hd = d // n_heads
    n_mem = mk.shape[0]
    return pl.pallas_call(
        functools.partial(_cross_shared_kernel, hd ** -0.5),
        grid=(m // tm, n_heads),
        in_specs=[
            pl.BlockSpec((tm, d), lambda i, j: (i, 0), pipeline_mode=pl.Buffered(1)),
            pl.BlockSpec((None, 1, d), lambda i, j: (layer, 0, 0)),
            pl.BlockSpec((None, d, hd), lambda i, j: (layer, 0, j)),
            pl.BlockSpec((n_mem, hd), lambda i, j: (0, j)),
            pl.BlockSpec((n_mem, hd), lambda i, j: (0, j)),
            pl.BlockSpec((None, hd, d), lambda i, j: (layer, j, 0)),
        ],
        out_specs=pl.BlockSpec((tm, d), lambda i, j: (i, 0)),
        out_shape=jax.ShapeDtypeStruct((m, d), F32),
        scratch_shapes=[pltpu.VMEM((tm, d), BF16)],
        compiler_params=_params("arbitrary", "arbitrary"),
        name="cross_shared_residual",
    )(x, g3, wq3, mk, mv, wo3)


def _upper_ones(n):
    r = lax.broadcasted_iota(jnp.int32, (n, n), 0)
    c = lax.broadcasted_iota(jnp.int32, (n, n), 1)
    return jnp.where(c > r, 1.0, 0.0).astype(BF16)


def _sb_chunk(zt, carry, tri, keep):
    sp = _softplus(zt)
    spm = sp if keep is None else jnp.where(keep, sp, 0.0)
    later = _dot(tri, spm.astype(BF16)) + carry
    w = jnp.exp(zt - sp - later)
    if keep is not None:
        w = jnp.where(keep, w, 0.0)
    return w, carry + jnp.sum(spm, axis=0, keepdims=True)


def _sb_prompt_kernel(layer, n_chunks, b_ref, q_ref, k_ref, v_ref, o_ref, kb_ref, vt_ref, acc_ref):
    h = pl.program_id(0)
    i = pl.program_id(1)
    tq, dh = q_ref.shape
    ck = SB_KEY_CHUNK
    per_q = tq // ck

    @pl.when(i == 0)
    def _():
        def prep(c, _):
            r = pl.multiple_of(c * ck, ck)
            kb_ref[c] = k_ref[pl.ds(r, ck), :].astype(BF16)
            vt_ref[c] = v_ref[pl.ds(r, ck), :].T.astype(BF16)
            return 0
        lax.fori_loop(0, n_chunks, prep, 0)

    q = (q_ref[...] * dh ** -0.5).astype(BF16)
    bias = b_ref[layer, h]
    tri = _upper_ones(ck)
    acc_ref[...] = jnp.zeros_like(acc_ref)

    def chunk(c, carry, keep):
        zt = _dot_nt(kb_ref[c], q) + bias
        w, carry = _sb_chunk(zt, carry, tri, keep)
        acc_ref[...] += _dot(vt_ref[c], w.astype(BF16))
        return carry

    carry = jnp.zeros((1, tq), F32)
    k_in = lax.broadcasted_iota(jnp.int32, (ck, tq), 0)
    q_in = lax.broadcasted_iota(jnp.int32, (ck, tq), 1)
    for d in reversed(range(per_q)):
        carry = chunk(i * per_q + d, carry, k_in + d * ck < q_in)

    def below(it, carry):
        base = (i - 1 - it) * per_q
        for d in reversed(range(per_q)):
            carry = chunk(base + d, carry, None)
        return carry

    lax.fori_loop(0, i, below, carry)
    o_ref[...] = acc_ref[...].T.astype(o_ref.dtype)


def sb_prompt(p, b_sb, layer, n_heads, *, tq):
    s = p.shape[0]
    dh = SB_HEAD_DIM
    n_chunks = s // SB_KEY_CHUNK
    return pl.pallas_call(
        functools.partial(_sb_prompt_kernel, layer, n_chunks),
        grid=(n_heads, s // tq),
        in_specs=[
            pl.BlockSpec(memory_space=pltpu.SMEM),
            pl.BlockSpec((tq, dh), lambda h, i: (i, h)),
            pl.BlockSpec((s, dh), lambda h, i: (0, n_heads + h)),
            pl.BlockSpec((s, dh), lambda h, i: (0, 2 * n_heads + h)),
        ],
        out_specs=pl.BlockSpec((tq, dh), lambda h, i: (i, h)),
        out_shape=jax.ShapeDtypeStruct((s, n_heads * dh), BF16),
        scratch_shapes=[
            pltpu.VMEM((n_chunks, SB_KEY_CHUNK, dh), BF16),
            pltpu.VMEM((n_chunks, dh, SB_KEY_CHUNK), BF16),
            pltpu.VMEM((dh, tq), F32),
        ],
        compiler_params=_params("arbitrary", "arbitrary"),
        name="sb_prompt",
    )(b_sb, p, p, p)


DECODE_LANES = 128
DECODE_ROWS_PER_T = 8
DECODE_ROWS_SHIFT = 3


def _log2(n):
    assert n & (n - 1) == 0
    return n.bit_length() - 1


def _block_diag_queries(q_ref, n_t, n_heads, scale):
    d = q_ref.shape[-1]
    hd_shift = _log2(d // n_heads)
    r = lax.broadcasted_iota(jnp.int32, (DECODE_LANES, d), 0)
    c = lax.broadcasted_iota(jnp.int32, (DECODE_LANES, d), 1)
    t_of_r = r >> DECODE_ROWS_SHIFT
    h_of_r = r & (DECODE_ROWS_PER_T - 1)
    out = jnp.zeros((DECODE_LANES, d), F32)
    for t in range(n_t):
        out = jnp.where(t_of_r == t, q_ref[t:t + 1, :] * scale, out)
    return jnp.where(((c >> hd_shift) == h_of_r) & (t_of_r < n_t), out, 0.0).astype(BF16)


def _take_block_diag(full, o_ref, n_t, n_heads):
    d = full.shape[-1]
    hd_shift = _log2(d // n_heads)
    r = lax.broadcasted_iota(jnp.int32, (DECODE_ROWS_PER_T, d), 0)
    c = lax.broadcasted_iota(jnp.int32, (DECODE_ROWS_PER_T, d), 1)
    own = (c >> hd_shift) == r
    for t in range(n_t):
        rows = full[t * DECODE_ROWS_PER_T:(t + 1) * DECODE_ROWS_PER_T, :]
        o_ref[t:t + 1, :] = jnp.sum(jnp.where(own, rows, 0.0), axis=0, keepdims=True).astype(o_ref.dtype)


def _heads_to_lanes(page_refs, n_heads):
    page = page_refs[0].shape[0] // n_heads
    cols = [jnp.concatenate([r[pl.ds(h, page, stride=n_heads), :] for r in page_refs], axis=0)
            for h in range(n_heads)]
    return jnp.concatenate(cols, axis=1)


def _sb_sample_kernel(n_pages, n_heads, pt_ref, bl_ref, q_ref, kn_ref, vn_ref, *refs):
    k_pages = refs[:n_pages]
    v_pages = refs[n_pages:2 * n_pages]
    o_ref, kx_ref, vx_ref = refs[2 * n_pages:]
    n_t, d = q_ref.shape
    ck = SB_KEY_CHUNK
    page = k_pages[0].shape[0] // n_heads
    pages_per_chunk = ck // page

    @pl.when(pl.program_id(0) == 0)
    def _():
        kx_ref[...] = jnp.zeros_like(kx_ref)
        vx_ref[...] = jnp.zeros_like(vx_ref)

    kx_ref[0:n_t, :] = kn_ref[...]
    vx_ref[0:n_t, :] = vn_ref[...]

    qbd = _block_diag_queries(q_ref, n_t, n_heads, (d // n_heads) ** -0.5)
    bias = bl_ref[...]
    tri = _upper_ones(ck)

    def chunk(kc, vc, carry, keep):
        zt = _dot_nt(kc.astype(BF16), qbd) + bias
        w, carry = _sb_chunk(zt, carry, tri, keep)
        return _dot(w.T.astype(BF16), vc.astype(BF16)), carry

    k_in = lax.broadcasted_iota(jnp.int32, (ck, DECODE_LANES), 0)
    t_of_lane = lax.broadcasted_iota(jnp.int32, (ck, DECODE_LANES), 1) >> DECODE_ROWS_SHIFT
    carry = jnp.zeros((1, DECODE_LANES), F32)
    full, carry = chunk(kx_ref[...], vx_ref[...], carry, k_in < t_of_lane)
    for c in reversed(range(n_pages // pages_per_chunk)):
        ps = range(c * pages_per_chunk, (c + 1) * pages_per_chunk)
        kc = _heads_to_lanes([k_pages[p] for p in ps], n_heads)
        vc = _heads_to_lanes([v_pages[p] for p in ps], n_heads)
        part, carry = chunk(kc, vc, carry, None)
        full = full + part
    _take_block_diag(full, o_ref, n_t, n_heads)


def sb_sample(p3, cache_k, cache_v, page_table, bias_lanes, layer, n_heads):
    db, n_t, _ = p3.shape
    n_pages = page_table.shape[1]
    page_rows, dh = cache_k.shape[2:]
    page = page_rows // n_heads
    d = n_heads * dh
    assert (n_pages * page) % SB_KEY_CHUNK == 0 and SB_KEY_CHUNK % page == 0
    assert n_t * DECODE_ROWS_PER_T <= DECODE_LANES and n_heads <= DECODE_ROWS_PER_T

    def page_spec(pg):
        return pl.BlockSpec((None, None, page_rows, dh), lambda b, pt: (layer, pt[b, pg], 0, 0))

    grid_spec = pltpu.PrefetchScalarGridSpec(
        num_scalar_prefetch=1,
        grid=(db,),
        in_specs=[
            pl.BlockSpec((None, 1, DECODE_LANES), lambda b, pt: (layer, 0, 0)),
            pl.BlockSpec((None, n_t, d), lambda b, pt: (b, 0, 0)),
            pl.BlockSpec((None, n_t, d), lambda b, pt: (b, 0, 1)),
            pl.BlockSpec((None, n_t, d), lambda b, pt: (b, 0, 2)),
        ] + [page_spec(pg) for pg in range(n_pages)] * 2,
        out_specs=pl.BlockSpec((None, n_t, d), lambda b, pt: (b, 0, 0)),
        scratch_shapes=[pltpu.VMEM((SB_KEY_CHUNK, d), F32), pltpu.VMEM((SB_KEY_CHUNK, d), F32)],
    )
    return pl.pallas_call(
        functools.partial(_sb_sample_kernel, n_pages, n_heads),
        grid_spec=grid_spec,
        out_shape=jax.ShapeDtypeStruct((db, n_t, d), F32),
        compiler_params=_params("arbitrary"),
        name="sb_sample",
    )(page_table, bias_lanes, p3, p3, p3, *([cache_k] * n_pages), *([cache_v] * n_pages))


def _cross_sample_kernel(q_ref, mk_ref, mv_ref, o_ref):
    n_t, d = q_ref.shape
    n_heads = mk_ref.shape[1]
    qbd = _block_diag_queries(q_ref, n_t, n_heads, (d // n_heads) ** -0.5)
    mk = jnp.concatenate([mk_ref[:, h, :] for h in range(n_heads)], axis=1)
    mv = jnp.concatenate([mv_ref[:, h, :] for h in range(n_heads)], axis=1)
    st = _dot_nt(mk.astype(BF16), qbd)
    e = jnp.exp(st - jnp.max(st, axis=0, keepdims=True))
    p = e / jnp.sum(e, axis=0, keepdims=True)
    full = _dot(p.T.astype(BF16), mv.astype(BF16))
    _take_block_diag(full, o_ref, n_t, n_heads)


def cross_sample(q3, mem_k, mem_v, layer):
    db, n_t, d = q3.shape
    n_mem, n_heads, hd = mem_k.shape[2:]
    assert n_t * DECODE_ROWS_PER_T <= DECODE_LANES and n_heads <= DECODE_ROWS_PER_T
    mem_spec = pl.BlockSpec((None, None, n_mem, n_heads, hd), lambda b: (layer, b, 0, 0, 0))
    return pl.pallas_call(
        _cross_sample_kernel,
        grid=(db,),
        in_specs=[pl.BlockSpec((None, n_t, d), lambda b: (b, 0, 0)), mem_spec, mem_spec],
        out_specs=pl.BlockSpec((None, n_t, d), lambda b: (b, 0, 0)),
        out_shape=jax.ShapeDtypeStruct((db, n_t, d), F32),
        compiler_params=_params("arbitrary"),
        name="cross_sample",
    )(q3, mem_k, mem_v)


def _ln_swish(c, g, b):
    mu = jnp.mean(c, axis=-1, keepdims=True)
    xc = c - mu
    var = jnp.mean(xc * xc, axis=-1, keepdims=True)
    y = xc * lax.rsqrt(var + EPS) * g + b
    return y * _sigmoid(y)


def _conv_prompt_kernel(width, halo, a_ref, gt_ref, ah_ref, gh_ref, w_ref, bd_ref, g_ref, b_ref,
                        o_ref, tail_ref, u_ref):
    i = pl.program_id(0)
    tb = a_ref.shape[0]
    u_halo = ah_ref[...] * _sigmoid(gh_ref[...])
    u_ref[0:halo, :] = jnp.where(i == 0, 0.0, u_halo)
    u_ref[halo:halo + tb, :] = a_ref[...] * _sigmoid(gt_ref[...])
    first = halo - (width - 1)
    rc = CONV_ROW_CHUNK

    for r0 in range(0, tb, rc):
        acc = jnp.broadcast_to(bd_ref[...], (rc, bd_ref.shape[-1]))
        for w in range(width):
            acc = acc + u_ref[r0 + first + w:r0 + first + w + rc, :] * w_ref[w:w + 1, :]
        o_ref[r0:r0 + rc, :] = _ln_swish(acc, g_ref[...], b_ref[...]).astype(o_ref.dtype)

    @pl.when(i == pl.num_programs(0) - 1)
    def _():
        tail_ref[...] = u_ref[tb:tb + halo, :]


def conv_prompt(p, w_dw, b_dw3, g3, b3, layer, *, tb):
    s = p.shape[0]
    width, c = w_dw.shape[1:]
    halo = 32
    assert width - 1 <= halo and tb % halo == 0
    hb = tb // halo
    vec = pl.BlockSpec((None, 1, c), lambda i: (layer, 0, 0))
    return pl.pallas_call(
        functools.partial(_conv_prompt_kernel, width, halo),
        grid=(s // tb,),
        in_specs=[
            pl.BlockSpec((tb, c), lambda i: (i, 3)),
            pl.BlockSpec((tb, c), lambda i: (i, 4)),
            pl.BlockSpec((halo, c), lambda i: (jnp.maximum(i * hb - 1, 0), 3)),
            pl.BlockSpec((halo, c), lambda i: (jnp.maximum(i * hb - 1, 0), 4)),
            pl.BlockSpec((None, width, c), lambda i: (layer, 0, 0)),
            vec, vec, vec,
        ],
        out_specs=[pl.BlockSpec((tb, c), lambda i: (i, 0)), pl.BlockSpec((halo, c), lambda i: (0, 0))],
        out_shape=[jax.ShapeDtypeStruct((s, c), BF16), jax.ShapeDtypeStruct((halo, c), F32)],
        scratch_shapes=[pltpu.VMEM((tb + halo, c), F32)],
        compiler_params=_params("arbitrary"),
        name="conv_prompt",
    )(p, p, p, p, w_dw, b_dw3, g3, b3)


def _conv_sample_kernel(width, a_ref, gt_ref, hist_ref, w_ref, bd_ref, g_ref, b_ref,
                        o_ref, hist_out_ref, u_ref, w_pad_ref):
    bb, n_t, c = a_ref.shape
    n_hist = width - 1
    taps = w_pad_ref.shape[0]
    w_pad_ref[0:width, :] = w_ref[...]
    w_pad_ref[width:taps, :] = jnp.zeros((taps - width, c), F32)
    u_ref[n_hist + n_t:, :] = jnp.zeros((u_ref.shape[0] - n_hist - n_t, c), F32)
    for b in range(bb):
        u_ref[0:n_hist, :] = hist_ref[b]
        u_ref[n_hist:n_hist + n_t, :] = a_ref[b] * _sigmoid(gt_ref[b])
        for t in range(n_t):
            conv = jnp.sum(u_ref[t:t + taps, :] * w_pad_ref[...], axis=0, keepdims=True) + bd_ref[...]
            o_ref[b, t:t + 1, :] = _ln_swish(conv, g_ref[...], b_ref[...])
        hist_out_ref[b] = u_ref[n_t:n_t + n_hist, :]


def conv_sample(p3, hist, w_dw, b_dw3, g3, b3, layer, *, bb):
    db, n_t, _ = p3.shape
    width, c = w_dw.shape[1:]
    n_hist = width - 1
    taps = 32
    assert width <= taps
    vec = pl.BlockSpec((None, 1, c), lambda i: (layer, 0, 0))
    return pl.pallas_call(
        functools.partial(_conv_sample_kernel, width),
        grid=(db // bb,),
        in_specs=[
            pl.BlockSpec((bb, n_t, c), lambda i: (i, 0, 3)),
            pl.BlockSpec((bb, n_t, c), lambda i: (i, 0, 4)),
            pl.BlockSpec((None, bb, n_hist, c), lambda i: (layer, i, 0, 0)),
            pl.BlockSpec((None, width, c), lambda i: (layer, 0, 0)),
            vec, vec, vec,
        ],
        out_specs=[pl.BlockSpec((bb, n_t, c), lambda i: (i, 0, 0)),
                   pl.BlockSpec((bb, n_hist, c), lambda i: (i, 0, 0))],
        out_shape=[jax.ShapeDtypeStruct((db, n_t, c), F32), jax.ShapeDtypeStruct((db, n_hist, c), F32)],
        scratch_shapes=[pltpu.VMEM((n_t + taps + 4, c), F32), pltpu.VMEM((taps, c), F32)],
        compiler_params=_params("arbitrary"),
        name="conv_sample",
    )(p3, p3, hist, w_dw, b_dw3, g3, b3)


def _rmsnorm_kernel(x_ref, g_ref, o_ref):
    x = x_ref[...]
    o_ref[...] = x * lax.rsqrt(jnp.mean(x * x, axis=-1, keepdims=True) + EPS) * g_ref[...]


def rmsnorm(x, g2, *, tm):
    m, d = x.shape
    return pl.pallas_call(
        _rmsnorm_kernel,
        grid=(m // tm,),
        in_specs=[pl.BlockSpec((tm, d), lambda i: (i, 0)), pl.BlockSpec((1, d), lambda i: (0, 0))],
        out_specs=pl.BlockSpec((tm, d), lambda i: (i, 0)),
        out_shape=jax.ShapeDtypeStruct((m, d), F32),
        compiler_params=_params("arbitrary"),
        name="rmsnorm",
    )(x, g2)


def kernel(x_prompt, x_sample, cache_sb_k, cache_sb_v, state_conv, cache_mem_k, cache_mem_v, page_table, mem_prompt, g_mix, w_in, b_sb, w_dw, b_dw, g_cln, b_cln, w_out, g_cross, g_mem, w_xq, w_xk, w_xv, w_xo, g_ffn, w_gate, w_up, w_down, g_final):
    batch, seq, d = x_prompt.shape
    db, n_t, _ = x_sample.shape
    depth = w_in.shape[0]
    n_sb_heads = b_sb.shape[1]
    sb_dim = n_sb_heads * SB_HEAD_DIM
    conv_dim = w_dw.shape[2]
    n_hist = w_dw.shape[1] - 1
    n_mem, n_x_heads, x_hd = cache_mem_k.shape[2:]
    n_pool, page = cache_sb_k.shape[1:3]
    assert batch == 1 and w_in.shape[2] == 3 * sb_dim + 2 * conv_dim and sb_dim == conv_dim
    ms = db * n_t

    col = lambda a: a.reshape(depth, 1, a.shape[-1])
    g_mix3, g_cross3, g_mem3, g_ffn3 = col(g_mix), col(g_cross), col(g_mem), col(g_ffn)
    b_dw3, g_cln3, b_cln3 = col(b_dw), col(g_cln), col(b_cln)
    cache_k = cache_sb_k.reshape(depth, n_pool, page * n_sb_heads, SB_HEAD_DIM)
    cache_v = cache_sb_v.reshape(depth, n_pool, page * n_sb_heads, SB_HEAD_DIM)
    lane_h = jnp.arange(DECODE_LANES) % DECODE_ROWS_PER_T
    bias_lanes = jnp.where(lane_h < n_sb_heads, b_sb[:, jnp.minimum(lane_h, n_sb_heads - 1)], 0.0)
    bias_lanes = bias_lanes.reshape(depth, 1, DECODE_LANES).astype(F32)

    tm_p = 1024
    xp = x_prompt.reshape(seq, d)
    xs = x_sample.reshape(ms, d)
    mem = mem_prompt.reshape(n_mem, d)
    skp, svp, sks, svs, cvp, cvs, mkp, mvp = [], [], [], [], [], [], [], []
    for l in range(depth):
        p = norm_matmul(xp, g_mix3, w_in, l, tm=tm_p, tn=512)
        o_sb = sb_prompt(p, b_sb, l, n_sb_heads, tq=PROMPT_Q_BLOCK)
        c, tail = conv_prompt(p, w_dw, b_dw3, g_cln3, b_cln3, l, tb=256)
        xp = matmul_residual([o_sb, c], w_out, l, xp, tm=tm_p, tn=512)
        skp.append(p[:, sb_dim:2 * sb_dim])
        svp.append(p[:, 2 * sb_dim:3 * sb_dim])
        cvp.append(tail[tail.shape[0] - n_hist:])

        ps = norm_matmul(xs, g_mix3, w_in, l, tm=ms, tn=512)
        ps3 = ps.reshape(db, n_t, ps.shape[-1])
        o_sb = sb_sample(ps3, cache_k, cache_v, page_table, bias_lanes, l, n_sb_heads)
        c, hist = conv_sample(ps3, state_conv, w_dw, b_dw3, g_cln3, b_cln3, l, bb=8)
        xs = matmul_residual([o_sb.reshape(ms, sb_dim), c.reshape(ms, conv_dim)], w_out, l, xs, tm=ms, tn=512)
        sks.append(ps[:, sb_dim:2 * sb_dim])
        svs.append(ps[:, 2 * sb_dim:3 * sb_dim])
        cvs.append(hist)

        mk = norm_matmul(mem, g_mem3, w_xk, l, tm=n_mem, tn=512)
        mv = norm_matmul(mem, g_mem3, w_xv, l, tm=n_mem, tn=512)
        mkp.append(mk)
        mvp.append(mv)
        xp = cross_shared_residual(xp, g_cross3, w_xq, mk, mv, w_xo, l, n_x_heads, tm=tm_p)
        qs = norm_matmul(xs, g_cross3, w_xq, l, tm=ms, tn=512)
        o_x = cross_sample(qs.reshape(db, n_t, d), cache_mem_k, cache_mem_v, l)
        xs = matmul_residual([o_x.reshape(ms, d)], w_xo, l, xs, tm=ms, tn=512)

        xp = ffn_residual(xp, g_ffn3, w_gate, w_up, w_down, l, tm=tm_p, tf=256)
        xs = ffn_residual(xs, g_ffn3, w_gate, w_up, w_down, l, tm=ms, tf=256)

    g_final2 = g_final.reshape(1, d)
    y_prompt = rmsnorm(xp, g_final2, tm=512).reshape(batch, seq, d)
    y_sample = rmsnorm(xs, g_final2, tm=ms).reshape(db, n_t, d)
    sb_shape = lambda rows: (depth,) + rows + (n_sb_heads, SB_HEAD_DIM)
    return (
        y_prompt,
        y_sample,
        jnp.stack(skp).reshape(sb_shape((batch, seq))),
        jnp.stack(svp).reshape(sb_shape((batch, seq))),
        jnp.stack(sks).reshape(sb_shape((db, n_t))),
        jnp.stack(svs).reshape(sb_shape((db, n_t))),
        jnp.stack(cvp).reshape(depth, batch, n_hist, conv_dim),
        jnp.stack(cvs),
        jnp.stack(mkp).reshape(depth, batch, n_mem, n_x_heads, x_hd),
        jnp.stack(mvp).reshape(depth, batch, n_mem, n_x_heads, x_hd),
    )
```

```python
import functools

import jax
import jax.numpy as jnp
from jax import lax
from jax.experimental import pallas as pl
from jax.experimental.pallas import tpu as pltpu

F32 = jnp.float32
BF16 = jnp.bfloat16
EPS = 1e-6

V7X_VMEM_LIMIT_BYTES = 56 * 1024 * 1024
SUBLANES = 8
SB_HEAD_DIM = 128
SB_KEY_CHUNK = 256
PROMPT_Q_BLOCK = 1024
CONV_ROW_CHUNK = 32


def _params(*semantics):
    return pltpu.CompilerParams(dimension_semantics=semantics, vmem_limit_bytes=V7X_VMEM_LIMIT_BYTES)


def _rms_bf16(x, g):
    y = x * lax.rsqrt(jnp.mean(x * x, axis=-1, keepdims=True) + EPS)
    return (y * g).astype(BF16)


def _sigmoid(x):
    return 1.0 / (1.0 + jnp.exp(-x))


def _neg_abs(z):
    sign = jnp.uint32(0x80000000)
    return lax.bitcast_convert_type(lax.bitcast_convert_type(z, jnp.uint32) | sign, F32)


def _softplus(z):
    return jnp.maximum(z, 0.0) + jnp.log(1.0 + jnp.exp(_neg_abs(z)))


def _dot(a, b):
    return jnp.dot(a, b, preferred_element_type=F32)


def _dot_nt(a, b):
    return lax.dot_general(a, b, (((1,), (1,)), ((), ())), preferred_element_type=F32)


def _norm_matmul_kernel(x_ref, g_ref, w_ref, o_ref, h_ref):
    @pl.when(pl.program_id(1) == 0)
    def _():
        h_ref[...] = _rms_bf16(x_ref[...], g_ref[...])

    o_ref[...] = _dot(h_ref[...], w_ref[...].astype(BF16)).astype(o_ref.dtype)


def norm_matmul(x, g3, w3, layer, *, tm, tn):
    m, k = x.shape
    n = w3.shape[-1]
    return pl.pallas_call(
        _norm_matmul_kernel,
        grid=(m // tm, n // tn),
        in_specs=[
            pl.BlockSpec((tm, k), lambda i, j: (i, 0), pipeline_mode=pl.Buffered(1)),
            pl.BlockSpec((None, 1, k), lambda i, j: (layer, 0, 0)),
            pl.BlockSpec((None, k, tn), lambda i, j: (layer, 0, j)),
        ],
        out_specs=pl.BlockSpec((tm, tn), lambda i, j: (i, j)),
        out_shape=jax.ShapeDtypeStruct((m, n), F32),
        scratch_shapes=[pltpu.VMEM((tm, k), BF16)],
        compiler_params=_params("arbitrary", "arbitrary"),
        name="norm_matmul",
    )(x, g3, w3)


def _matmul_residual_kernel(n_pieces, *refs):
    a_refs = refs[:n_pieces]
    w_refs = refs[n_pieces:2 * n_pieces]
    r_ref, o_ref = refs[2 * n_pieces:]
    acc = r_ref[...]
    for a_ref, w_ref in zip(a_refs, w_refs):
        acc = acc + _dot(a_ref[...].astype(BF16), w_ref[...].astype(BF16))
    o_ref[...] = acc


def matmul_residual(pieces, w3, layer, res, *, tm, tn):
    m = res.shape[0]
    n = w3.shape[-1]
    kp = pieces[0].shape[-1]
    n_pieces = len(pieces)
    in_specs = [pl.BlockSpec((tm, kp), lambda i, j: (i, 0)) for _ in pieces]
    in_specs += [pl.BlockSpec((None, kp, tn), functools.partial(lambda p, i, j: (layer, p, j), p))
                 for p in range(n_pieces)]
    in_specs += [pl.BlockSpec((tm, tn), lambda i, j: (i, j))]
    return pl.pallas_call(
        functools.partial(_matmul_residual_kernel, n_pieces),
        grid=(m // tm, n // tn),
        in_specs=in_specs,
        out_specs=pl.BlockSpec((tm, tn), lambda i, j: (i, j)),
        out_shape=jax.ShapeDtypeStruct((m, n), F32),
        compiler_params=_params("arbitrary", "arbitrary"),
        name="matmul_residual",
    )(*pieces, *([w3] * n_pieces), res)


def _ffn_kernel(x_ref, g_ref, wg_ref, wu_ref, wd_ref, o_ref, h_ref):
    @pl.when(pl.program_id(1) == 0)
    def _():
        x = x_ref[...]
        h_ref[...] = _rms_bf16(x, g_ref[...])
        o_ref[...] = x

    h = h_ref[...]
    gate = _dot(h, wg_ref[...].astype(BF16))
    up = _dot(h, wu_ref[...].astype(BF16))
    act = (gate * _sigmoid(gate) * up).astype(BF16)
    o_ref[...] += _dot(act, wd_ref[...].astype(BF16))


def ffn_residual(x, g3, wg3, wu3, wd3, layer, *, tm, tf):
    m, d = x.shape
    f = wg3.shape[-1]
    return pl.pallas_call(
        _ffn_kernel,
        grid=(m // tm, f // tf),
        in_specs=[
            pl.BlockSpec((tm, d), lambda i, j: (i, 0), pipeline_mode=pl.Buffered(1)),
            pl.BlockSpec((None, 1, d), lambda i, j: (layer, 0, 0)),
            pl.BlockSpec((None, d, tf), lambda i, j: (layer, 0, j)),
            pl.BlockSpec((None, d, tf), lambda i, j: (layer, 0, j)),
            pl.BlockSpec((None, tf, d), lambda i, j: (layer, j, 0)),
        ],
        out_specs=pl.BlockSpec((tm, d), lambda i, j: (i, 0)),
        out_shape=jax.ShapeDtypeStruct((m, d), F32),
        scratch_shapes=[pltpu.VMEM((tm, d), BF16)],
        compiler_params=_params("arbitrary", "arbitrary"),
        name="ffn_residual",
    )(x, g3, wg3, wu3, wd3)


def _cross_shared_kernel(scale, x_ref, g_ref, wq_ref, mk_ref, mv_ref, wo_ref, o_ref, h_ref):
    @pl.when(pl.program_id(1) == 0)
    def _():
        x = x_ref[...]
        h_ref[...] = _rms_bf16(x, g_ref[...])
        o_ref[...] = x

    q = _dot(h_ref[...], wq_ref[...].astype(BF16))
    s = _dot_nt(q.astype(BF16), mk_ref[...].astype(BF16)) * scale
    e = jnp.exp(s - jnp.max(s, axis=-1, keepdims=True))
    p = e / jnp.sum(e, axis=-1, keepdims=True)
    o = _dot(p.astype(BF16), mv_ref[...].astype(BF16))
    o_ref[...] += _dot(o.astype(BF16), wo_ref[...].astype(BF16))


def cross_shared_residual(x, g3, wq3, mk, mv, wo3, layer, n_heads, *, tm):
    m, d = x.shape
    hd = d // n_heads
    n_mem = mk.shape[0]
    return pl.pallas_call(
        functools.partial(_cross_shared_kernel, hd ** -0.5),
        grid=(m // tm, n_heads),
        in_specs=[
            pl.BlockSpec((tm, d), lambda i, j: (i, 0), pipeline_mode=pl.Buffered(1)),
            pl.BlockSpec((None, 1, d), lambda i, j: (layer, 0, 0)),
            pl.BlockSpec((None, d, hd), lambda i, j: (layer, 0, j)),
            pl.BlockSpec((n_mem, hd), lambda i, j: (0, j)),
            pl.BlockSpec((n_mem, hd), lambda i, j: (0, j)),
            pl.BlockSpec((None, hd, d), lambda i, j: (layer, j, 0)),
        ],
        out_specs=pl.BlockSpec((tm, d), lambda i, j: (i, 0)),
        out_shape=jax.ShapeDtypeStruct((m, d), F32),
        scratch_shapes=[pltpu.VMEM((tm, d), BF16)],
        compiler_params=_params("arbitrary", "arbitrary"),
        name="cross_shared_residual",
    )(x, g3, wq3, mk, mv, wo3)


def _upper_ones(n):
    r = lax.broadcasted_iota(jnp.int32, (n, n), 0)
    c = lax.broadcasted_iota(jnp.int32, (n, n), 1)
    return jnp.where(c > r, 1.0, 0.0).astype(BF16)


def _sb_chunk(zt, carry, tri, keep):
    sp = _softplus(zt)
    spm = sp if keep is None else jnp.where(keep, sp, 0.0)
    within = _dot(tri, spm.astype(BF16))
    w = jnp.exp((zt - sp) - (within + carry))
    if keep is not None:
        w = jnp.where(keep, w, 0.0)
    return w, carry + within[0:1, :] + spm[0:1, :]


def _sb_prompt_kernel(layer, n_chunks, b_ref, q_ref, k_ref, v_ref, o_ref, kb_ref, vt_ref, acc_ref, carry_ref):
    h = pl.program_id(0)
    i = pl.program_id(1)
    tq, dh = q_ref.shape
    ck = SB_KEY_CHUNK
    per_q = tq // ck

    @pl.when(i == 0)
    def _():
        def prep(c, _):
            r = pl.multiple_of(c * ck, ck)
            kb_ref[c] = k_ref[pl.ds(r, ck), :].astype(BF16)
            vt_ref[c] = v_ref[pl.ds(r, ck), :].T.astype(BF16)
            return 0
        lax.fori_loop(0, n_chunks, prep, 0)

    q = (q_ref[...] * dh ** -0.5).astype(BF16)
    bias = b_ref[layer, h]
    tri = _upper_ones(ck)
    acc_ref[...] = jnp.zeros_like(acc_ref)
    carry_ref[...] = jnp.zeros_like(carry_ref)

    for d in reversed(range(per_q)):
        lo = d * ck
        keep = (lax.broadcasted_iota(jnp.int32, (ck, tq - lo), 0)
                < lax.broadcasted_iota(jnp.int32, (ck, tq - lo), 1))
        c = i * per_q + d
        zt = _dot_nt(kb_ref[c], q[lo:, :]) + bias
        w, carry = _sb_chunk(zt, carry_ref[:, lo:], tri, keep)
        carry_ref[:, lo:] = carry
        acc_ref[:, lo:] += _dot(vt_ref[c], w.astype(BF16))

    def below(it, _):
        base = (i - 1 - it) * per_q
        carry = carry_ref[...]
        pv = None
        for d in reversed(range(per_q)):
            zt = _dot_nt(kb_ref[base + d], q) + bias
            w, carry = _sb_chunk(zt, carry, tri, None)
            part = _dot(vt_ref[base + d], w.astype(BF16))
            pv = part if pv is None else pv + part
        carry_ref[...] = carry
        acc_ref[...] += pv
        return 0

    lax.fori_loop(0, i, below, 0)
    o_ref[...] = acc_ref[...].T.astype(o_ref.dtype)


def sb_prompt(p, b_sb, layer, n_heads, *, tq):
    s = p.shape[0]
    dh = SB_HEAD_DIM
    n_chunks = s // SB_KEY_CHUNK
    return pl.pallas_call(
        functools.partial(_sb_prompt_kernel, layer, n_chunks),
        grid=(n_heads, s // tq),
        in_specs=[
            pl.BlockSpec(memory_space=pltpu.SMEM),
            pl.BlockSpec((tq, dh), lambda h, i: (i, h)),
            pl.BlockSpec((s, dh), lambda h, i: (0, n_heads + h)),
            pl.BlockSpec((s, dh), lambda h, i: (0, 2 * n_heads + h)),
        ],
        out_specs=pl.BlockSpec((tq, dh), lambda h, i: (i, h)),
        out_shape=jax.ShapeDtypeStruct((s, n_heads * dh), BF16),
        scratch_shapes=[
            pltpu.VMEM((n_chunks, SB_KEY_CHUNK, dh), BF16),
            pltpu.VMEM((n_chunks, dh, SB_KEY_CHUNK), BF16),
            pltpu.VMEM((dh, tq), F32),
            pltpu.VMEM((1, tq), F32),
        ],
        compiler_params=_params("arbitrary", "arbitrary"),
        name="sb_prompt",
    )(b_sb, p, p, p)


DECODE_LANES = 128
DECODE_ROWS_PER_T = 8
DECODE_ROWS_SHIFT = 3


def _log2(n):
    assert n & (n - 1) == 0
    return n.bit_length() - 1


def _block_diag_queries(q_ref, n_t, n_heads, scale):
    d = q_ref.shape[-1]
    hd_shift = _log2(d // n_heads)
    r = lax.broadcasted_iota(jnp.int32, (DECODE_LANES, d), 0)
    c = lax.broadcasted_iota(jnp.int32, (DECODE_LANES, d), 1)
    t_of_r = r >> DECODE_ROWS_SHIFT
    h_of_r = r & (DECODE_ROWS_PER_T - 1)
    out = jnp.zeros((DECODE_LANES, d), F32)
    for t in range(n_t):
        out = jnp.where(t_of_r == t, q_ref[t:t + 1, :] * scale, out)
    return jnp.where(((c >> hd_shift) == h_of_r) & (t_of_r < n_t), out, 0.0)


def _take_block_diag(full, o_ref, n_t, n_heads):
    d = full.shape[-1]
    hd_shift = _log2(d // n_heads)
    r = lax.broadcasted_iota(jnp.int32, (DECODE_ROWS_PER_T, d), 0)
    c = lax.broadcasted_iota(jnp.int32, (DECODE_ROWS_PER_T, d), 1)
    own = (c >> hd_shift) == r
    for t in range(n_t):
        rows = full[t * DECODE_ROWS_PER_T:(t + 1) * DECODE_ROWS_PER_T, :]
        o_ref[t:t + 1, :] = jnp.sum(jnp.where(own, rows, 0.0), axis=0, keepdims=True).astype(o_ref.dtype)


NEW_KEY_ROWS = 16


def _sb_sample_kernel(n_pages, n_heads, pt_ref, bl_ref, q_ref, kn_ref, vn_ref, *refs):
    k_pages = refs[:n_pages]
    v_pages = refs[n_pages:2 * n_pages]
    o_ref, kf_ref, vf_ref, kx_ref, vx_ref = refs[2 * n_pages:]
    n_t, d = q_ref.shape
    dh = d // n_heads
    ck = SB_KEY_CHUNK
    page = k_pages[0].shape[0] // n_heads
    n_past = n_pages * page // ck
    new0 = n_past * ck

    @pl.when(pl.program_id(0) == 0)
    def _():
        kx_ref[...] = jnp.zeros_like(kx_ref)
        vx_ref[...] = jnp.zeros_like(vx_ref)
        kf_ref[new0:, :] = jnp.zeros((ck, d), BF16)
        vf_ref[new0:, :] = jnp.zeros((ck, d), BF16)

    def relayout(src_refs, dst_ref):
        for pg, src in enumerate(src_refs):
            for h in range(n_heads):
                dst_ref[pg * page:(pg + 1) * page, h * dh:(h + 1) * dh] = (
                    src[pl.ds(h, page, stride=n_heads), :].astype(BF16))

    kx_ref[0:n_t, :] = kn_ref[...]
    kf_ref[new0:new0 + NEW_KEY_ROWS, :] = kx_ref[...].astype(BF16)
    relayout(k_pages, kf_ref)
    qbd_t = _block_diag_queries(q_ref, n_t, n_heads, dh ** -0.5).T.astype(BF16)
    zt = _dot(kf_ref[...], qbd_t) + bl_ref[...]

    vx_ref[0:n_t, :] = vn_ref[...]
    vf_ref[new0:new0 + NEW_KEY_ROWS, :] = vx_ref[...].astype(BF16)
    relayout(v_pages, vf_ref)

    rows = lax.broadcasted_iota(jnp.int32, zt.shape, 0)
    t_of_lane = lax.broadcasted_iota(jnp.int32, zt.shape, 1) >> DECODE_ROWS_SHIFT
    keep = rows - new0 < t_of_lane
    sp = _softplus(zt)
    spm = jnp.where(keep, sp, 0.0)
    zs = zt - sp
    tri = _upper_ones(ck)
    within = [_dot(tri, spm[c * ck:(c + 1) * ck].astype(BF16)) for c in range(n_past + 1)]
    carry = jnp.zeros((1, DECODE_LANES), F32)
    pieces = [None] * (n_past + 1)
    for c in reversed(range(n_past + 1)):
        lo = c * ck
        w = jnp.exp(zs[lo:lo + ck] - (within[c] + carry))
        if c == n_past:
            w = jnp.where(keep[lo:lo + ck], w, 0.0)
        pieces[c] = w.T.astype(BF16)
        carry = carry + within[c][0:1, :] + spm[lo:lo + 1, :]
    full = _dot(jnp.concatenate(pieces, axis=1), vf_ref[...])
    _take_block_diag(full, o_ref, n_t, n_heads)


def sb_sample(p3, cache_k, cache_v, page_table, bias_lanes, layer, n_heads):
    db, n_t, _ = p3.shape
    n_pages = page_table.shape[1]
    page_rows, dh = cache_k.shape[2:]
    page = page_rows // n_heads
    d = n_heads * dh
    assert (n_pages * page) % SB_KEY_CHUNK == 0 and SB_KEY_CHUNK % page == 0
    assert n_t * DECODE_ROWS_PER_T <= DECODE_LANES and n_heads <= DECODE_ROWS_PER_T and n_t <= NEW_KEY_ROWS
    n_chunks = n_pages * page // SB_KEY_CHUNK

    def page_spec(pg):
        return pl.BlockSpec((None, None, page_rows, dh), lambda b, pt: (layer, pt[b, pg], 0, 0))

    grid_spec = pltpu.PrefetchScalarGridSpec(
        num_scalar_prefetch=1,
        grid=(db,),
        in_specs=[
            pl.BlockSpec((None, 1, DECODE_LANES), lambda b, pt: (layer, 0, 0)),
            pl.BlockSpec((None, n_t, d), lambda b, pt: (b, 0, 0)),
            pl.BlockSpec((None, n_t, d), lambda b, pt: (b, 0, 1)),
            pl.BlockSpec((None, n_t, d), lambda b, pt: (b, 0, 2)),
        ] + [page_spec(pg) for pg in range(n_pages)] * 2,
        out_specs=pl.BlockSpec((None, n_t, d), lambda b, pt: (b, 0, 0)),
        scratch_shapes=[
            pltpu.VMEM(((n_chunks + 1) * SB_KEY_CHUNK, d), BF16),
            pltpu.VMEM(((n_chunks + 1) * SB_KEY_CHUNK, d), BF16),
            pltpu.VMEM((NEW_KEY_ROWS, d), F32),
            pltpu.VMEM((NEW_KEY_ROWS, d), F32),
        ],
    )
    return pl.pallas_call(
        functools.partial(_sb_sample_kernel, n_pages, n_heads),
        grid_spec=grid_spec,
        out_shape=jax.ShapeDtypeStruct((db, n_t, d), F32),
        compiler_params=_params("arbitrary"),
        name="sb_sample",
    )(page_table, bias_lanes, p3, p3, p3, *([cache_k] * n_pages), *([cache_v] * n_pages))


def _cross_sample_kernel(q_ref, mk_ref, mv_ref, o_ref):
    n_t, d = q_ref.shape
    n_heads = mk_ref.shape[1]
    qbd = _block_diag_queries(q_ref, n_t, n_heads, (d // n_heads) ** -0.5).astype(BF16)
    mk = jnp.concatenate([mk_ref[:, h, :] for h in range(n_heads)], axis=1)
    mv = jnp.concatenate([mv_ref[:, h, :] for h in range(n_heads)], axis=1)
    st = _dot_nt(mk.astype(BF16), qbd)
    e = jnp.exp(st - jnp.max(st, axis=0, keepdims=True))
    p = e / jnp.sum(e, axis=0, keepdims=True)
    full = _dot(p.T.astype(BF16), mv.astype(BF16))
    _take_block_diag(full, o_ref, n_t, n_heads)


def cross_sample(q3, mem_k, mem_v, layer):
    db, n_t, d = q3.shape
    n_mem, n_heads, hd = mem_k.shape[2:]
    assert n_t * DECODE_ROWS_PER_T <= DECODE_LANES and n_heads <= DECODE_ROWS_PER_T
    mem_spec = pl.BlockSpec((None, None, n_mem, n_heads, hd), lambda b: (layer, b, 0, 0, 0))
    return pl.pallas_call(
        _cross_sample_kernel,
        grid=(db,),
        in_specs=[pl.BlockSpec((None, n_t, d), lambda b: (b, 0, 0)), mem_spec, mem_spec],
        out_specs=pl.BlockSpec((None, n_t, d), lambda b: (b, 0, 0)),
        out_shape=jax.ShapeDtypeStruct((db, n_t, d), F32),
        compiler_params=_params("arbitrary"),
        name="cross_sample",
    )(q3, mem_k, mem_v)


def _ln_swish(c, g, b):
    mu = jnp.mean(c, axis=-1, keepdims=True)
    xc = c - mu
    var = jnp.mean(xc * xc, axis=-1, keepdims=True)
    y = xc * lax.rsqrt(var + EPS) * g + b
    return y * _sigmoid(y)


def _conv_prompt_kernel(width, halo, a_ref, gt_ref, ah_ref, gh_ref, w_ref, bd_ref, g_ref, b_ref,
                        o_ref, tail_ref, u_ref):
    i = pl.program_id(0)
    tb = a_ref.shape[0]
    u_halo = ah_ref[...] * _sigmoid(gh_ref[...])
    u_ref[0, 0:halo, :] = jnp.where(i == 0, 0.0, u_halo)
    u_ref[0, halo:halo + tb, :] = a_ref[...] * _sigmoid(gt_ref[...])
    n_rows = halo + tb - SUBLANES
    for s in range(1, SUBLANES):
        u_ref[s, 0:n_rows, :] = u_ref[0, s:s + n_rows, :]
    first = halo - (width - 1)
    rc = CONV_ROW_CHUNK

    for r0 in range(0, tb, rc):
        acc = jnp.broadcast_to(bd_ref[...], (rc, bd_ref.shape[-1]))
        for w in range(width):
            s, base = (first + w) % SUBLANES, (first + w) // SUBLANES * SUBLANES
            acc = acc + u_ref[s, r0 + base:r0 + base + rc, :] * w_ref[w:w + 1, :]
        o_ref[r0:r0 + rc, :] = _ln_swish(acc, g_ref[...], b_ref[...]).astype(o_ref.dtype)

    @pl.when(i == pl.num_programs(0) - 1)
    def _():
        tail_ref[...] = u_ref[0, tb:tb + halo, :]


def conv_prompt(p, w_dw, b_dw3, g3, b3, layer, *, tb):
    s = p.shape[0]
    width, c = w_dw.shape[1:]
    halo = 32
    assert width - 1 <= halo and tb % halo == 0
    hb = tb // halo
    vec = pl.BlockSpec((None, 1, c), lambda i: (layer, 0, 0))
    return pl.pallas_call(
        functools.partial(_conv_prompt_kernel, width, halo),
        grid=(s // tb,),
        in_specs=[
            pl.BlockSpec((tb, c), lambda i: (i, 3)),
            pl.BlockSpec((tb, c), lambda i: (i, 4)),
            pl.BlockSpec((halo, c), lambda i: (jnp.maximum(i * hb - 1, 0), 3)),
            pl.BlockSpec((halo, c), lambda i: (jnp.maximum(i * hb - 1, 0), 4)),
            pl.BlockSpec((None, width, c), lambda i: (layer, 0, 0)),
            vec, vec, vec,
        ],
        out_specs=[pl.BlockSpec((tb, c), lambda i: (i, 0)), pl.BlockSpec((halo, c), lambda i: (0, 0))],
        out_shape=[jax.ShapeDtypeStruct((s, c), BF16), jax.ShapeDtypeStruct((halo, c), F32)],
        scratch_shapes=[pltpu.VMEM((SUBLANES, tb + halo, c), F32)],
        compiler_params=_params("arbitrary"),
        name="conv_prompt",
    )(p, p, p, p, w_dw, b_dw3, g3, b3)


def _conv_sample_kernel(width, a_ref, gt_ref, hist_ref, w_ref, bd_ref, g_ref, b_ref,
                        o_ref, hist_out_ref, u_ref, w_pad_ref):
    bb, n_t, c = a_ref.shape
    n_hist = width - 1
    taps = w_pad_ref.shape[0]
    w_pad_ref[0:width, :] = w_ref[...]
    w_pad_ref[width:taps, :] = jnp.zeros((taps - width, c), F32)
    u_ref[n_hist + n_t:, :] = jnp.zeros((u_ref.shape[0] - n_hist - n_t, c), F32)
    for b in range(bb):
        u_ref[0:n_hist, :] = hist_ref[b]
        u_ref[n_hist:n_hist + n_t, :] = a_ref[b] * _sigmoid(gt_ref[b])
        for t in range(n_t):
            conv = jnp.sum(u_ref[t:t + taps, :] * w_pad_ref[...], axis=0, keepdims=True) + bd_ref[...]
            o_ref[b, t:t + 1, :] = _ln_swish(conv, g_ref[...], b_ref[...])
        hist_out_ref[b] = u_ref[n_t:n_t + n_hist, :]


def conv_sample(p3, hist, w_dw, b_dw3, g3, b3, layer, *, bb):
    db, n_t, _ = p3.shape
    width, c = w_dw.shape[1:]
    n_hist = width - 1
    taps = 32
    assert width <= taps
    vec = pl.BlockSpec((None, 1, c), lambda i: (layer, 0, 0))
    return pl.pallas_call(
        functools.partial(_conv_sample_kernel, width),
        grid=(db // bb,),
        in_specs=[
            pl.BlockSpec((bb, n_t, c), lambda i: (i, 0, 3)),
            pl.BlockSpec((bb, n_t, c), lambda i: (i, 0, 4)),
            pl.BlockSpec((None, bb, n_hist, c), lambda i: (layer, i, 0, 0)),
            pl.BlockSpec((None, width, c), lambda i: (layer, 0, 0)),
            vec, vec, vec,
        ],
        out_specs=[pl.BlockSpec((bb, n_t, c), lambda i: (i, 0, 0)),
                   pl.BlockSpec((bb, n_hist, c), lambda i: (i, 0, 0))],
        out_shape=[jax.ShapeDtypeStruct((db, n_t, c), F32), jax.ShapeDtypeStruct((db, n_hist, c), F32)],
        scratch_shapes=[pltpu.VMEM((n_t + taps + 4, c), F32), pltpu.VMEM((taps, c), F32)],
        compiler_params=_params("arbitrary"),
        name="conv_sample",
    )(p3, p3, hist, w_dw, b_dw3, g3, b3)


def _rmsnorm_kernel(x_ref, g_ref, o_ref):
    x = x_ref[...]
    o_ref[...] = x * lax.rsqrt(jnp.mean(x * x, axis=-1, keepdims=True) + EPS) * g_ref[...]


def rmsnorm(x, g2, *, tm):
    m, d = x.shape
    return pl.pallas_call(
        _rmsnorm_kernel,
        grid=(m // tm,),
        in_specs=[pl.BlockSpec((tm, d), lambda i: (i, 0)), pl.BlockSpec((1, d), lambda i: (0, 0))],
        out_specs=pl.BlockSpec((tm, d), lambda i: (i, 0)),
        out_shape=jax.ShapeDtypeStruct((m, d), F32),
        compiler_params=_params("arbitrary"),
        name="rmsnorm",
    )(x, g2)


def kernel(x_prompt, x_sample, cache_sb_k, cache_sb_v, state_conv, cache_mem_k, cache_mem_v, page_table, mem_prompt, g_mix, w_in, b_sb, w_dw, b_dw, g_cln, b_cln, w_out, g_cross, g_mem, w_xq, w_xk, w_xv, w_xo, g_ffn, w_gate, w_up, w_down, g_final):
    batch, seq, d = x_prompt.shape
    db, n_t, _ = x_sample.shape
    depth = w_in.shape[0]
    n_sb_heads = b_sb.shape[1]
    sb_dim = n_sb_heads * SB_HEAD_DIM
    conv_dim = w_dw.shape[2]
    n_hist = w_dw.shape[1] - 1
    n_mem, n_x_heads, x_hd = cache_mem_k.shape[2:]
    n_pool, page = cache_sb_k.shape[1:3]
    assert batch == 1 and w_in.shape[2] == 3 * sb_dim + 2 * conv_dim and sb_dim == conv_dim
    ms = db * n_t

    col = lambda a: a.reshape(depth, 1, a.shape[-1])
    g_mix3, g_cross3, g_mem3, g_ffn3 = col(g_mix), col(g_cross), col(g_mem), col(g_ffn)
    b_dw3, g_cln3, b_cln3 = col(b_dw), col(g_cln), col(b_cln)
    cache_k = cache_sb_k.reshape(depth, n_pool, page * n_sb_heads, SB_HEAD_DIM)
    cache_v = cache_sb_v.reshape(depth, n_pool, page * n_sb_heads, SB_HEAD_DIM)
    lane_h = jnp.arange(DECODE_LANES) % DECODE_ROWS_PER_T
    bias_lanes = jnp.where(lane_h < n_sb_heads, b_sb[:, jnp.minimum(lane_h, n_sb_heads - 1)], 0.0)
    bias_lanes = bias_lanes.reshape(depth, 1, DECODE_LANES).astype(F32)

    tm_p = 1024
    xp = x_prompt.reshape(seq, d)
    xs = x_sample.reshape(ms, d)
    mem = mem_prompt.reshape(n_mem, d)
    skp, svp, sks, svs, cvp, cvs, mkp, mvp = [], [], [], [], [], [], [], []
    for l in range(depth):
        p = norm_matmul(xp, g_mix3, w_in, l, tm=2 * tm_p, tn=512)
        o_sb = sb_prompt(p, b_sb, l, n_sb_heads, tq=PROMPT_Q_BLOCK)
        c, tail = conv_prompt(p, w_dw, b_dw3, g_cln3, b_cln3, l, tb=256)
        xp = matmul_residual([o_sb, c], w_out, l, xp, tm=2 * tm_p, tn=512)
        skp.append(p[:, sb_dim:2 * sb_dim])
        svp.append(p[:, 2 * sb_dim:3 * sb_dim])
        cvp.append(tail[tail.shape[0] - n_hist:])

        ps = norm_matmul(xs, g_mix3, w_in, l, tm=ms, tn=512)
        ps3 = ps.reshape(db, n_t, ps.shape[-1])
        o_sb = sb_sample(ps3, cache_k, cache_v, page_table, bias_lanes, l, n_sb_heads)
        c, hist = conv_sample(ps3, state_conv, w_dw, b_dw3, g_cln3, b_cln3, l, bb=8)
        xs = matmul_residual([o_sb.reshape(ms, sb_dim), c.reshape(ms, conv_dim)], w_out, l, xs, tm=ms, tn=512)
        sks.append(ps[:, sb_dim:2 * sb_dim])
        svs.append(ps[:, 2 * sb_dim:3 * sb_dim])
        cvs.append(hist)

        mk = norm_matmul(mem, g_mem3, w_xk, l, tm=n_mem, tn=512)
        mv = norm_matmul(mem, g_mem3, w_xv, l, tm=n_mem, tn=512)
        mkp.append(mk)
        mvp.append(mv)
        xp = cross_shared_residual(xp, g_cross3, w_xq, mk, mv, w_xo, l, n_x_heads, tm=tm_p)
        qs = norm_matmul(xs, g_cross3, w_xq, l, tm=ms, tn=512)
        o_x = cross_sample(qs.reshape(db, n_t, d), cache_mem_k, cache_mem_v, l)
        xs = matmul_residual([o_x.reshape(ms, d)], w_xo, l, xs, tm=ms, tn=512)

        xp = ffn_residual(xp, g_ffn3, w_gate, w_up, w_down, l, tm=tm_p, tf=256)
        xs = ffn_residual(xs, g_ffn3, w_gate, w_up, w_down, l, tm=ms, tf=256)

    g_final2 = g_final.reshape(1, d)
    y_prompt = rmsnorm(xp, g_final2, tm=512).reshape(batch, seq, d)
    y_sample = rmsnorm(xs, g_final2, tm=ms).reshape(db, n_t, d)
    sb_shape = lambda rows: (depth,) + rows + (n_sb_heads, SB_HEAD_DIM)
    return (
        y_prompt,
        y_sample,
        jnp.stack(skp).reshape(sb_shape((batch, seq))),
        jnp.stack(svp).reshape(sb_shape((batch, seq))),
        jnp.stack(sks).reshape(sb_shape((db, n_t))),
        jnp.stack(svs).reshape(sb_shape((db, n_t))),
        jnp.stack(cvp).reshape(depth, batch, n_hist, conv_dim),
        jnp.stack(cvs),
        jnp.stack(mkp).reshape(depth, batch, n_mem, n_x_heads, x_hd),
        jnp.stack(mvp).reshape(depth, batch, n_mem, n_x_heads, x_hd),
    )
```

```python
import functools

import jax
import jax.numpy as jnp
from jax import lax
from jax.experimental import pallas as pl
from jax.experimental.pallas import tpu as pltpu

F32 = jnp.float32
BF16 = jnp.bfloat16
EPS = 1e-6

V7X_VMEM_LIMIT_BYTES = 56 * 1024 * 1024
SUBLANES = 8
SB_HEAD_DIM = 128
SB_KEY_CHUNK = 256
PROMPT_Q_BLOCK = 1024
CONV_ROW_CHUNK = 32


def _params(*semantics):
    return pltpu.CompilerParams(dimension_semantics=semantics, vmem_limit_bytes=V7X_VMEM_LIMIT_BYTES)


def _rms(x, g):
    return x * lax.rsqrt(jnp.mean(x * x, axis=-1, keepdims=True) + EPS) * g


def _rms_bf16(x, g):
    return _rms(x, g).astype(BF16)


def _sigmoid(x):
    return 1.0 / (1.0 + jnp.exp(-x))


def _neg_abs(z):
    sign = jnp.uint32(0x80000000)
    return lax.bitcast_convert_type(lax.bitcast_convert_type(z, jnp.uint32) | sign, F32)


def _softplus(z):
    return jnp.maximum(z, 0.0) + jnp.log(1.0 + jnp.exp(_neg_abs(z)))


def _dot(a, b):
    return jnp.dot(a, b, preferred_element_type=F32)


def _dot_nt(a, b):
    return lax.dot_general(a, b, (((1,), (1,)), ((), ())), preferred_element_type=F32)


def _norm_matmul_kernel(x_ref, g_ref, w_ref, o_ref, h_ref):
    @pl.when(pl.program_id(1) == 0)
    def _():
        h_ref[...] = _rms_bf16(x_ref[...], g_ref[...])

    o_ref[...] = _dot(h_ref[...], w_ref[...].astype(BF16)).astype(o_ref.dtype)


def norm_matmul(x, g3, w3, layer, *, tm, tn):
    m, k = x.shape
    n = w3.shape[-1]
    return pl.pallas_call(
        _norm_matmul_kernel,
        grid=(m // tm, n // tn),
        in_specs=[
            pl.BlockSpec((tm, k), lambda i, j: (i, 0), pipeline_mode=pl.Buffered(1)),
            pl.BlockSpec((None, 1, k), lambda i, j: (layer, 0, 0)),
            pl.BlockSpec((None, k, tn), lambda i, j: (layer, 0, j)),
        ],
        out_specs=pl.BlockSpec((tm, tn), lambda i, j: (i, j)),
        out_shape=jax.ShapeDtypeStruct((m, n), F32),
        scratch_shapes=[pltpu.VMEM((tm, k), BF16)],
        compiler_params=_params("arbitrary", "arbitrary"),
        name="norm_matmul",
    )(x, g3, w3)


def _matmul_residual_kernel(n_pieces, *refs):
    a_refs = refs[:n_pieces]
    w_refs = refs[n_pieces:2 * n_pieces]
    r_ref, o_ref = refs[2 * n_pieces:]
    acc = r_ref[...]
    for a_ref, w_ref in zip(a_refs, w_refs):
        acc = acc + _dot(a_ref[...].astype(BF16), w_ref[...].astype(BF16))
    o_ref[...] = acc


def matmul_residual(pieces, w3, layer, res, *, tm, tn):
    m = res.shape[0]
    n = w3.shape[-1]
    kp = pieces[0].shape[-1]
    n_pieces = len(pieces)
    in_specs = [pl.BlockSpec((tm, kp), lambda i, j: (i, 0)) for _ in pieces]
    in_specs += [pl.BlockSpec((None, kp, tn), functools.partial(lambda p, i, j: (layer, p, j), p))
                 for p in range(n_pieces)]
    in_specs += [pl.BlockSpec((tm, tn), lambda i, j: (i, j))]
    return pl.pallas_call(
        functools.partial(_matmul_residual_kernel, n_pieces),
        grid=(m // tm, n // tn),
        in_specs=in_specs,
        out_specs=pl.BlockSpec((tm, tn), lambda i, j: (i, j)),
        out_shape=jax.ShapeDtypeStruct((m, n), F32),
        compiler_params=_params("arbitrary", "arbitrary"),
        name="matmul_residual",
    )(*pieces, *([w3] * n_pieces), res)


def _ffn_kernel(final_norm, x_ref, g_ref, wg_ref, wu_ref, wd_ref, *refs):
    gf_ref = refs[0] if final_norm else None
    o_ref, h_ref = refs[-2:]

    @pl.when(pl.program_id(1) == 0)
    def _():
        x = x_ref[...]
        h_ref[...] = _rms_bf16(x, g_ref[...])
        o_ref[...] = x

    h = h_ref[...]
    gate = _dot(h, wg_ref[...].astype(BF16))
    up = _dot(h, wu_ref[...].astype(BF16))
    act = (gate * _sigmoid(gate) * up).astype(BF16)
    o_ref[...] += _dot(act, wd_ref[...].astype(BF16))

    if final_norm:
        @pl.when(pl.program_id(1) == pl.num_programs(1) - 1)
        def _():
            o_ref[...] = _rms(o_ref[...], gf_ref[...])


def ffn_residual(x, g3, wg3, wu3, wd3, layer, *, tm, tf, g_final2=None):
    m, d = x.shape
    f = wg3.shape[-1]
    final_norm = g_final2 is not None
    in_specs = [
        pl.BlockSpec((tm, d), lambda i, j: (i, 0), pipeline_mode=pl.Buffered(1)),
        pl.BlockSpec((None, 1, d), lambda i, j: (layer, 0, 0)),
        pl.BlockSpec((None, d, tf), lambda i, j: (layer, 0, j)),
        pl.BlockSpec((None, d, tf), lambda i, j: (layer, 0, j)),
        pl.BlockSpec((None, tf, d), lambda i, j: (layer, j, 0)),
    ]
    args = [x, g3, wg3, wu3, wd3]
    if final_norm:
        in_specs.append(pl.BlockSpec((1, d), lambda i, j: (0, 0)))
        args.append(g_final2)
    return pl.pallas_call(
        functools.partial(_ffn_kernel, final_norm),
        grid=(m // tm, f // tf),
        in_specs=in_specs,
        out_specs=pl.BlockSpec((tm, d), lambda i, j: (i, 0)),
        out_shape=jax.ShapeDtypeStruct((m, d), F32),
        scratch_shapes=[pltpu.VMEM((tm, d), BF16)],
        compiler_params=_params("arbitrary", "arbitrary"),
        name="ffn_residual",
    )(*args)


def _cross_shared_kernel(scale, x_ref, g_ref, wq_ref, mk_ref, mv_ref, wo_ref, o_ref, h_ref):
    @pl.when(pl.program_id(1) == 0)
    def _():
        x = x_ref[...]
        h_ref[...] = _rms_bf16(x, g_ref[...])
        o_ref[...] = x

    q = _dot(h_ref[...], wq_ref[...].astype(BF16))
    s = _dot_nt(q.astype(BF16), mk_ref[...].astype(BF16)) * scale
    e = jnp.exp(s - jnp.max(s, axis=-1, keepdims=True))
    p = e / jnp.sum(e, axis=-1, keepdims=True)
    o = _dot(p.astype(BF16), mv_ref[...].astype(BF16))
    o_ref[...] += _dot(o.astype(BF16), wo_ref[...].astype(BF16))


def cross_shared_residual(x, g3, wq3, mk, mv, wo3, layer, n_heads, *, tm):
    m, d = x.shape
    hd = d // n_heads
    n_mem = mk.shape[0]
    return pl.pallas_call(
        functools.partial(_cross_shared_kernel, hd ** -0.5),
        grid=(m // tm, n_heads),
        in_specs=[
            pl.BlockSpec((tm, d), lambda i, j: (i, 0), pipeline_mode=pl.Buffered(1)),
            pl.BlockSpec((None, 1, d), lambda i, j: (layer, 0, 0)),
            pl.BlockSpec((None, d, hd), lambda i, j: (layer, 0, j)),
            pl.BlockSpec((n_mem, hd), lambda i, j: (0, j)),
            pl.BlockSpec((n_mem, hd), lambda i, j: (0, j)),
            pl.BlockSpec((None, hd, d), lambda i, j: (layer, j, 0)),
        ],
        out_specs=pl.BlockSpec((tm, d), lambda i, j: (i, 0)),
        out_shape=jax.ShapeDtypeStruct((m, d), F32),
        scratch_shapes=[pltpu.VMEM((tm, d), BF16)],
        compiler_params=_params("arbitrary", "arbitrary"),
        name="cross_shared_residual",
    )(x, g3, wq3, mk, mv, wo3)


def _upper_ones(n):
    r = lax.broadcasted_iota(jnp.int32, (n, n), 0)
    c = lax.broadcasted_iota(jnp.int32, (n, n), 1)
    return jnp.where(c > r, 1.0, 0.0).astype(BF16)


def _sb_prompt_kernel(layer, n_chunks, b_ref, q_ref, k_ref, v_ref, o_ref, kb_ref, vt_ref, acc_ref, carry_ref):
    h = pl.program_id(0)
    i = pl.program_id(1)
    tq, dh = q_ref.shape
    ck = SB_KEY_CHUNK
    per_q = tq // ck

    @pl.when(i == 0)
    def _():
        def prep(c, _):
            r = pl.multiple_of(c * ck, ck)
            kb_ref[c] = k_ref[pl.ds(r, ck), :].astype(BF16)
            vt_ref[c] = v_ref[pl.ds(r, ck), :].T.astype(BF16)
            return 0
        lax.fori_loop(0, n_chunks, prep, 0)

    q_t = (q_ref[...] * dh ** -0.5).T.astype(BF16)
    bias = b_ref[layer, h]
    tri = _upper_ones(ck)
    order = list(reversed(range(per_q)))

    zts = [_dot(kb_ref[i * per_q + d], q_t[:, d * ck:]) + bias for d in order]
    keeps = [lax.broadcasted_iota(jnp.int32, zt.shape, 0) < lax.broadcasted_iota(jnp.int32, zt.shape, 1)
             for zt in zts]
    sps = [_softplus(zt) for zt in zts]
    spms = [jnp.where(keep, sp, 0.0) for keep, sp in zip(keeps, sps)]
    withins = [_dot(tri, spm.astype(BF16)) for spm in spms]
    carry = jnp.zeros((1, tq), F32)
    acc_ref[...] = jnp.zeros_like(acc_ref)
    for d, zt, keep, sp, spm, within in zip(order, zts, keeps, sps, spms, withins):
        lo = d * ck
        w = jnp.where(keep, jnp.exp((zt - sp) - (within + carry[:, lo:])), 0.0)
        seen = carry[:, lo:] + within[0:1, :] + spm[0:1, :]
        carry = seen if lo == 0 else jnp.concatenate([carry[:, :lo], seen], axis=1)
        acc_ref[:, lo:] += _dot(vt_ref[i * per_q + d], w.astype(BF16))
    carry_ref[...] = carry

    def below(it, _):
        base = (i - 1 - it) * per_q
        zts = [_dot(kb_ref[base + d], q_t) + bias for d in order]
        sps = [_softplus(zt) for zt in zts]
        withins = [_dot(tri, sp.astype(BF16)) for sp in sps]
        carry = carry_ref[...]
        pv = None
        for d, zt, sp, within in zip(order, zts, sps, withins):
            w = jnp.exp((zt - sp) - (within + carry))
            carry = carry + within[0:1, :] + sp[0:1, :]
            part = _dot(vt_ref[base + d], w.astype(BF16))
            pv = part if pv is None else pv + part
        carry_ref[...] = carry
        acc_ref[...] += pv
        return 0

    lax.fori_loop(0, i, below, 0)
    o_ref[...] = acc_ref[...].T.astype(o_ref.dtype)


def sb_prompt(p, b_sb, layer, n_heads, *, tq):
    s = p.shape[0]
    dh = SB_HEAD_DIM
    n_chunks = s // SB_KEY_CHUNK
    return pl.pallas_call(
        functools.partial(_sb_prompt_kernel, layer, n_chunks),
        grid=(n_heads, s // tq),
        in_specs=[
            pl.BlockSpec(memory_space=pltpu.SMEM),
            pl.BlockSpec((tq, dh), lambda h, i: (i, h)),
            pl.BlockSpec((s, dh), lambda h, i: (0, n_heads + h)),
            pl.BlockSpec((s, dh), lambda h, i: (0, 2 * n_heads + h)),
        ],
        out_specs=pl.BlockSpec((tq, dh), lambda h, i: (i, h)),
        out_shape=jax.ShapeDtypeStruct((s, n_heads * dh), BF16),
        scratch_shapes=[
            pltpu.VMEM((n_chunks, SB_KEY_CHUNK, dh), BF16),
            pltpu.VMEM((n_chunks, dh, SB_KEY_CHUNK), BF16),
            pltpu.VMEM((dh, tq), F32),
            pltpu.VMEM((1, tq), F32),
        ],
        compiler_params=_params("arbitrary", "arbitrary"),
        name="sb_prompt",
    )(b_sb, p, p, p)


DECODE_LANES = 128
DECODE_ROWS_PER_T = 8
DECODE_ROWS_SHIFT = 3
MASKED_SCORE = -1e30


def _log2(n):
    assert n & (n - 1) == 0
    return n.bit_length() - 1


def _block_diag_queries(q_ref, n_t, n_heads, scale):
    d = q_ref.shape[-1]
    hd_shift = _log2(d // n_heads)
    r = lax.broadcasted_iota(jnp.int32, (DECODE_LANES, d), 0)
    c = lax.broadcasted_iota(jnp.int32, (DECODE_LANES, d), 1)
    t_of_r = r >> DECODE_ROWS_SHIFT
    h_of_r = r & (DECODE_ROWS_PER_T - 1)
    out = jnp.zeros((DECODE_LANES, d), F32)
    for t in range(n_t):
        out = jnp.where(t_of_r == t, q_ref[t:t + 1, :] * scale, out)
    return jnp.where(((c >> hd_shift) == h_of_r) & (t_of_r < n_t), out, 0.0)


def _take_block_diag(full, o_ref, n_t, n_heads):
    d = full.shape[-1]
    hd_shift = _log2(d // n_heads)
    r = lax.broadcasted_iota(jnp.int32, (DECODE_ROWS_PER_T, d), 0)
    c = lax.broadcasted_iota(jnp.int32, (DECODE_ROWS_PER_T, d), 1)
    own = (c >> hd_shift) == r
    for t in range(n_t):
        rows = full[t * DECODE_ROWS_PER_T:(t + 1) * DECODE_ROWS_PER_T, :]
        o_ref[t:t + 1, :] = jnp.sum(jnp.where(own, rows, 0.0), axis=0, keepdims=True).astype(o_ref.dtype)


NEW_KEY_ROWS = 16


def _sb_sample_kernel(n_pages, n_heads, pt_ref, bl_ref, q_ref, kn_ref, vn_ref, *refs):
    k_pages = refs[:n_pages]
    v_pages = refs[n_pages:2 * n_pages]
    o_ref, kf_ref, vf_ref, kx_ref, vx_ref = refs[2 * n_pages:]
    n_t, d = q_ref.shape
    dh = d // n_heads
    ck = SB_KEY_CHUNK
    page = k_pages[0].shape[0] // n_heads
    n_past = n_pages * page // ck
    new0 = n_past * ck

    @pl.when(pl.program_id(0) == 0)
    def _():
        kx_ref[...] = jnp.zeros_like(kx_ref)
        vx_ref[...] = jnp.zeros_like(vx_ref)
        kf_ref[new0:, :] = jnp.zeros((ck, d), BF16)
        vf_ref[new0:, :] = jnp.zeros((ck, d), BF16)

    def relayout(src_refs, dst_ref):
        for pg, src in enumerate(src_refs):
            for h in range(n_heads):
                dst_ref[pg * page:(pg + 1) * page, h * dh:(h + 1) * dh] = (
                    src[pl.ds(h, page, stride=n_heads), :].astype(BF16))

    kx_ref[0:n_t, :] = kn_ref[...]
    kf_ref[new0:new0 + NEW_KEY_ROWS, :] = kx_ref[...].astype(BF16)
    vx_ref[0:n_t, :] = vn_ref[...]
    vf_ref[new0:new0 + NEW_KEY_ROWS, :] = vx_ref[...].astype(BF16)
    relayout(k_pages, kf_ref)
    relayout(v_pages, vf_ref)
    qbd_t = _block_diag_queries(q_ref, n_t, n_heads, dh ** -0.5).T.astype(BF16)
    half = (new0 + ck) // 2
    zt = jnp.concatenate([_dot(kf_ref[0:half, :], qbd_t), _dot(kf_ref[half:, :], qbd_t)], axis=0) + bl_ref[...]

    rows = lax.broadcasted_iota(jnp.int32, zt.shape, 0)
    t_of_lane = lax.broadcasted_iota(jnp.int32, zt.shape, 1) >> DECODE_ROWS_SHIFT
    keep = rows - new0 < t_of_lane
    sp = _softplus(zt)
    spm = jnp.where(keep, sp, 0.0)
    zs = zt - sp
    tri = _upper_ones(ck)
    within = [_dot(tri, spm[c * ck:(c + 1) * ck].astype(BF16)) for c in range(n_past + 1)]
    carry = jnp.zeros((1, DECODE_LANES), F32)
    pieces = [None] * (n_past + 1)
    for c in reversed(range(n_past + 1)):
        lo = c * ck
        w = jnp.exp(zs[lo:lo + ck] - (within[c] + carry))
        if c == n_past:
            w = jnp.where(keep[lo:lo + ck], w, 0.0)
        pieces[c] = w.T.astype(BF16)
        carry = carry + within[c][0:1, :] + spm[lo:lo + 1, :]
    full = _dot(jnp.concatenate(pieces, axis=1), vf_ref[...])
    _take_block_diag(full, o_ref, n_t, n_heads)


def sb_sample(p3, cache_k, cache_v, page_table, bias_lanes, layer, n_heads):
    db, n_t, _ = p3.shape
    n_pages = page_table.shape[1]
    page_rows, dh = cache_k.shape[2:]
    page = page_rows // n_heads
    d = n_heads * dh
    assert (n_pages * page) % SB_KEY_CHUNK == 0 and SB_KEY_CHUNK % page == 0
    assert n_t * DECODE_ROWS_PER_T <= DECODE_LANES and n_heads <= DECODE_ROWS_PER_T and n_t <= NEW_KEY_ROWS
    n_chunks = n_pages * page // SB_KEY_CHUNK

    def page_spec(pg):
        return pl.BlockSpec((None, None, page_rows, dh), lambda b, pt: (layer, pt[b, pg], 0, 0))

    grid_spec = pltpu.PrefetchScalarGridSpec(
        num_scalar_prefetch=1,
        grid=(db,),
        in_specs=[
            pl.BlockSpec((None, 1, DECODE_LANES), lambda b, pt: (layer, 0, 0)),
            pl.BlockSpec((None, n_t, d), lambda b, pt: (b, 0, 0)),
            pl.BlockSpec((None, n_t, d), lambda b, pt: (b, 0, 1)),
            pl.BlockSpec((None, n_t, d), lambda b, pt: (b, 0, 2)),
        ] + [page_spec(pg) for pg in range(n_pages)] * 2,
        out_specs=pl.BlockSpec((None, n_t, d), lambda b, pt: (b, 0, 0)),
        scratch_shapes=[
            pltpu.VMEM(((n_chunks + 1) * SB_KEY_CHUNK, d), BF16),
            pltpu.VMEM(((n_chunks + 1) * SB_KEY_CHUNK, d), BF16),
            pltpu.VMEM((NEW_KEY_ROWS, d), F32),
            pltpu.VMEM((NEW_KEY_ROWS, d), F32),
        ],
    )
    return pl.pallas_call(
        functools.partial(_sb_sample_kernel, n_pages, n_heads),
        grid_spec=grid_spec,
        out_shape=jax.ShapeDtypeStruct((db, n_t, d), F32),
        compiler_params=_params("arbitrary"),
        name="sb_sample",
    )(page_table, bias_lanes, p3, p3, p3, *([cache_k] * n_pages), *([cache_v] * n_pages))


def _cross_sample_kernel(q_ref, mk_ref, mv_ref, o_ref):
    n_t, d = q_ref.shape
    n_mem, n_heads, hd = mk_ref.shape
    n_rows = n_mem * n_heads
    q_rows = n_t * DECODE_ROWS_PER_T
    r = lax.broadcasted_iota(jnp.int32, (q_rows, hd), 0)
    qcat = jnp.zeros((q_rows, hd), F32)
    for t in range(n_t):
        for h in range(n_heads):
            qcat = jnp.where(r == t * DECODE_ROWS_PER_T + h, q_ref[t:t + 1, h * hd:(h + 1) * hd], qcat)
    qcat = jnp.concatenate([qcat * hd ** -0.5, jnp.zeros((DECODE_LANES - q_rows, hd), F32)], axis=0)
    mk2 = mk_ref[...].reshape(n_rows, hd).astype(BF16)
    mv2 = mv_ref[...].reshape(n_rows, hd).astype(BF16)
    st = _dot_nt(mk2, qcat.astype(BF16))
    row_h = lax.broadcasted_iota(jnp.int32, st.shape, 0) & (n_heads - 1)
    lane_h = lax.broadcasted_iota(jnp.int32, st.shape, 1) & (DECODE_ROWS_PER_T - 1)
    st = jnp.where(row_h == lane_h, st, MASKED_SCORE)
    e = jnp.exp(st - jnp.max(st, axis=0, keepdims=True))
    p = e / jnp.sum(e, axis=0, keepdims=True)
    out = _dot(p.T.astype(BF16), mv2)
    for t in range(n_t):
        for h in range(n_heads):
            row = t * DECODE_ROWS_PER_T + h
            o_ref[t:t + 1, h * hd:(h + 1) * hd] = out[row:row + 1, :]


def cross_sample(q3, mem_k, mem_v, layer):
    db, n_t, d = q3.shape
    n_mem, n_heads, hd = mem_k.shape[2:]
    assert n_t * DECODE_ROWS_PER_T <= DECODE_LANES and n_heads <= DECODE_ROWS_PER_T
    assert n_heads & (n_heads - 1) == 0
    mem_spec = pl.BlockSpec((None, None, n_mem, n_heads, hd), lambda b: (layer, b, 0, 0, 0))
    return pl.pallas_call(
        _cross_sample_kernel,
        grid=(db,),
        in_specs=[pl.BlockSpec((None, n_t, d), lambda b: (b, 0, 0)), mem_spec, mem_spec],
        out_specs=pl.BlockSpec((None, n_t, d), lambda b: (b, 0, 0)),
        out_shape=jax.ShapeDtypeStruct((db, n_t, d), F32),
        compiler_params=_params("arbitrary"),
        name="cross_sample",
    )(q3, mem_k, mem_v)


def _ln_swish(c, g, b):
    mu = jnp.mean(c, axis=-1, keepdims=True)
    xc = c - mu
    var = jnp.mean(xc * xc, axis=-1, keepdims=True)
    y = xc * lax.rsqrt(var + EPS) * g + b
    return y * _sigmoid(y)


def _conv_prompt_kernel(width, halo, a_ref, gt_ref, ah_ref, gh_ref, w_ref, bd_ref, g_ref, b_ref,
                        o_ref, tail_ref, u_ref):
    i = pl.program_id(0)
    tb = a_ref.shape[0]
    u_halo = ah_ref[...] * _sigmoid(gh_ref[...])
    u_ref[0, 0:halo, :] = jnp.where(i == 0, 0.0, u_halo)
    u_ref[0, halo:halo + tb, :] = a_ref[...] * _sigmoid(gt_ref[...])
    n_rows = halo + tb - SUBLANES
    for s in range(1, SUBLANES):
        u_ref[s, 0:n_rows, :] = u_ref[0, s:s + n_rows, :]
    first = halo - (width - 1)
    rc = CONV_ROW_CHUNK

    for r0 in range(0, tb, rc):
        acc = jnp.broadcast_to(bd_ref[...], (rc, bd_ref.shape[-1]))
        for w in range(width):
            s, base = (first + w) % SUBLANES, (first + w) // SUBLANES * SUBLANES
            acc = acc + u_ref[s, r0 + base:r0 + base + rc, :] * w_ref[w:w + 1, :]
        o_ref[r0:r0 + rc, :] = _ln_swish(acc, g_ref[...], b_ref[...]).astype(o_ref.dtype)

    @pl.when(i == pl.num_programs(0) - 1)
    def _():
        tail_ref[...] = u_ref[0, tb:tb + halo, :]


def conv_prompt(p, w_dw, b_dw3, g3, b3, layer, *, tb):
    s = p.shape[0]
    width, c = w_dw.shape[1:]
    halo = 32
    assert width - 1 <= halo and tb % halo == 0
    hb = tb // halo
    vec = pl.BlockSpec((None, 1, c), lambda i: (layer, 0, 0))
    return pl.pallas_call(
        functools.partial(_conv_prompt_kernel, width, halo),
        grid=(s // tb,),
        in_specs=[
            pl.BlockSpec((tb, c), lambda i: (i, 3)),
            pl.BlockSpec((tb, c), lambda i: (i, 4)),
            pl.BlockSpec((halo, c), lambda i: (jnp.maximum(i * hb - 1, 0), 3)),
            pl.BlockSpec((halo, c), lambda i: (jnp.maximum(i * hb - 1, 0), 4)),
            pl.BlockSpec((None, width, c), lambda i: (layer, 0, 0)),
            vec, vec, vec,
        ],
        out_specs=[pl.BlockSpec((tb, c), lambda i: (i, 0)), pl.BlockSpec((halo, c), lambda i: (0, 0))],
        out_shape=[jax.ShapeDtypeStruct((s, c), BF16), jax.ShapeDtypeStruct((halo, c), F32)],
        scratch_shapes=[pltpu.VMEM((SUBLANES, tb + halo, c), F32)],
        compiler_params=_params("arbitrary"),
        name="conv_prompt",
    )(p, p, p, p, w_dw, b_dw3, g3, b3)


def _conv_sample_kernel(width, a_ref, gt_ref, hist_ref, w_ref, bd_ref, g_ref, b_ref,
                        o_ref, hist_out_ref, u_ref, w_pad_ref):
    bb, n_t, c = a_ref.shape
    n_hist = width - 1
    taps = w_pad_ref.shape[0]
    w_pad_ref[0:width, :] = w_ref[...]
    w_pad_ref[width:taps, :] = jnp.zeros((taps - width, c), F32)
    u_ref[n_hist + n_t:, :] = jnp.zeros((u_ref.shape[0] - n_hist - n_t, c), F32)
    for b in range(bb):
        u_ref[0:n_hist, :] = hist_ref[b]
        u_ref[n_hist:n_hist + n_t, :] = a_ref[b] * _sigmoid(gt_ref[b])
        for t in range(n_t):
            conv = jnp.sum(u_ref[t:t + taps, :] * w_pad_ref[...], axis=0, keepdims=True) + bd_ref[...]
            o_ref[b, t:t + 1, :] = _ln_swish(conv, g_ref[...], b_ref[...])
        hist_out_ref[b] = u_ref[n_t:n_t + n_hist, :]


def conv_sample(p3, hist, w_dw, b_dw3, g3, b3, layer, *, bb):
    db, n_t, _ = p3.shape
    width, c = w_dw.shape[1:]
    n_hist = width - 1
    taps = 32
    assert width <= taps
    vec = pl.BlockSpec((None, 1, c), lambda i: (layer, 0, 0))
    return pl.pallas_call(
        functools.partial(_conv_sample_kernel, width),
        grid=(db // bb,),
        in_specs=[
            pl.BlockSpec((bb, n_t, c), lambda i: (i, 0, 3)),
            pl.BlockSpec((bb, n_t, c), lambda i: (i, 0, 4)),
            pl.BlockSpec((None, bb, n_hist, c), lambda i: (layer, i, 0, 0)),
            pl.BlockSpec((None, width, c), lambda i: (layer, 0, 0)),
            vec, vec, vec,
        ],
        out_specs=[pl.BlockSpec((bb, n_t, c), lambda i: (i, 0, 0)),
                   pl.BlockSpec((bb, n_hist, c), lambda i: (i, 0, 0))],
        out_shape=[jax.ShapeDtypeStruct((db, n_t, c), F32), jax.ShapeDtypeStruct((db, n_hist, c), F32)],
        scratch_shapes=[pltpu.VMEM((n_t + taps + 4, c), F32), pltpu.VMEM((taps, c), F32)],
        compiler_params=_params("arbitrary"),
        name="conv_sample",
    )(p3, p3, hist, w_dw, b_dw3, g3, b3)


def kernel(x_prompt, x_sample, cache_sb_k, cache_sb_v, state_conv, cache_mem_k, cache_mem_v, page_table, mem_prompt, g_mix, w_in, b_sb, w_dw, b_dw, g_cln, b_cln, w_out, g_cross, g_mem, w_xq, w_xk, w_xv, w_xo, g_ffn, w_gate, w_up, w_down, g_final):
    batch, seq, d = x_prompt.shape
    db, n_t, _ = x_sample.shape
    depth = w_in.shape[0]
    n_sb_heads = b_sb.shape[1]
    sb_dim = n_sb_heads * SB_HEAD_DIM
    conv_dim = w_dw.shape[2]
    n_hist = w_dw.shape[1] - 1
    n_mem, n_x_heads, x_hd = cache_mem_k.shape[2:]
    n_pool, page = cache_sb_k.shape[1:3]
    assert batch == 1 and w_in.shape[2] == 3 * sb_dim + 2 * conv_dim and sb_dim == conv_dim
    ms = db * n_t

    col = lambda a: a.reshape(depth, 1, a.shape[-1])
    g_mix3, g_cross3, g_mem3, g_ffn3 = col(g_mix), col(g_cross), col(g_mem), col(g_ffn)
    b_dw3, g_cln3, b_cln3 = col(b_dw), col(g_cln), col(b_cln)
    cache_k = cache_sb_k.reshape(depth, n_pool, page * n_sb_heads, SB_HEAD_DIM)
    cache_v = cache_sb_v.reshape(depth, n_pool, page * n_sb_heads, SB_HEAD_DIM)
    lane_h = jnp.arange(DECODE_LANES) % DECODE_ROWS_PER_T
    bias_lanes = jnp.where(lane_h < n_sb_heads, b_sb[:, jnp.minimum(lane_h, n_sb_heads - 1)], 0.0)
    bias_lanes = bias_lanes.reshape(depth, 1, DECODE_LANES).astype(F32)

    tm_p = 1024
    xp = x_prompt.reshape(seq, d)
    xs = x_sample.reshape(ms, d)
    mem = mem_prompt.reshape(n_mem, d)
    skp, svp, sks, svs, cvp, cvs, mkp, mvp = [], [], [], [], [], [], [], []
    for l in range(depth):
        p = norm_matmul(xp, g_mix3, w_in, l, tm=2 * tm_p, tn=512)
        o_sb = sb_prompt(p, b_sb, l, n_sb_heads, tq=PROMPT_Q_BLOCK)
        c, tail = conv_prompt(p, w_dw, b_dw3, g_cln3, b_cln3, l, tb=256)
        xp = matmul_residual([o_sb, c], w_out, l, xp, tm=2 * tm_p, tn=512)
        skp.append(p[:, sb_dim:2 * sb_dim])
        svp.append(p[:, 2 * sb_dim:3 * sb_dim])
        cvp.append(tail[tail.shape[0] - n_hist:])

        ps = norm_matmul(xs, g_mix3, w_in, l, tm=ms, tn=512)
        ps3 = ps.reshape(db, n_t, ps.shape[-1])
        o_sb = sb_sample(ps3, cache_k, cache_v, page_table, bias_lanes, l, n_sb_heads)
        c, hist = conv_sample(ps3, state_conv, w_dw, b_dw3, g_cln3, b_cln3, l, bb=8)
        xs = matmul_residual([o_sb.reshape(ms, sb_dim), c.reshape(ms, conv_dim)], w_out, l, xs, tm=ms, tn=512)
        sks.append(ps[:, sb_dim:2 * sb_dim])
        svs.append(ps[:, 2 * sb_dim:3 * sb_dim])
        cvs.append(hist)

        mk = norm_matmul(mem, g_mem3, w_xk, l, tm=n_mem, tn=512)
        mv = norm_matmul(mem, g_mem3, w_xv, l, tm=n_mem, tn=512)
        mkp.append(mk)
        mvp.append(mv)
        xp = cross_shared_residual(xp, g_cross3, w_xq, mk, mv, w_xo, l, n_x_heads, tm=tm_p)
        qs = norm_matmul(xs, g_cross3, w_xq, l, tm=ms, tn=512)
        o_x = cross_sample(qs.reshape(db, n_t, d), cache_mem_k, cache_mem_v, l)
        xs = matmul_residual([o_x.reshape(ms, d)], w_xo, l, xs, tm=ms, tn=512)

        g_last = g_final.reshape(1, d) if l == depth - 1 else None
        xp = ffn_residual(xp, g_ffn3, w_gate, w_up, w_down, l, tm=tm_p, tf=256, g_final2=g_last)
        xs = ffn_residual(xs, g_ffn3, w_gate, w_up, w_down, l, tm=ms, tf=256, g_final2=g_last)

    y_prompt = xp.reshape(batch, seq, d)
    y_sample = xs.reshape(db, n_t, d)
    sb_shape = lambda rows: (depth,) + rows + (n_sb_heads, SB_HEAD_DIM)
    return (
        y_prompt,
        y_sample,
        jnp.stack(skp).reshape(sb_shape((batch, seq))),
        jnp.stack(svp).reshape(sb_shape((batch, seq))),
        jnp.stack(sks).reshape(sb_shape((db, n_t))),
        jnp.stack(svs).reshape(sb_shape((db, n_t))),
        jnp.stack(cvp).reshape(depth, batch, n_hist, conv_dim),
        jnp.stack(cvs),
        jnp.stack(mkp).reshape(depth, batch, n_mem, n_x_heads, x_hd),
        jnp.stack(mvp).reshape(depth, batch, n_mem, n_x_heads, x_hd),
    )
```

```python
import functools

import jax
import jax.numpy as jnp
from jax import lax
from jax.experimental import pallas as pl
from jax.experimental.pallas import tpu as pltpu

F32 = jnp.float32
BF16 = jnp.bfloat16
EPS = 1e-6

V7X_VMEM_LIMIT_BYTES = 56 * 1024 * 1024
SUBLANES = 8
SB_HEAD_DIM = 128
SB_KEY_CHUNK = 256
PROMPT_Q_BLOCK = 1024
CONV_ROW_CHUNK = 32


def _params(*semantics):
    return pltpu.CompilerParams(dimension_semantics=semantics, vmem_limit_bytes=V7X_VMEM_LIMIT_BYTES)


def _rms(x, g):
    return x * lax.rsqrt(jnp.mean(x * x, axis=-1, keepdims=True) + EPS) * g


def _rms_bf16(x, g):
    return _rms(x, g).astype(BF16)


def _sigmoid(x):
    return 1.0 / (1.0 + jnp.exp(-x))


def _neg_abs(z):
    sign = jnp.uint32(0x80000000)
    return lax.bitcast_convert_type(lax.bitcast_convert_type(z, jnp.uint32) | sign, F32)


def _softplus(z):
    return jnp.maximum(z, 0.0) + jnp.log(1.0 + jnp.exp(_neg_abs(z)))


def _dot(a, b):
    return jnp.dot(a, b, preferred_element_type=F32)


def _dot_nt(a, b):
    return lax.dot_general(a, b, (((1,), (1,)), ((), ())), preferred_element_type=F32)


def _norm_matmul_kernel(x_ref, g_ref, w_ref, o_ref, h_ref):
    @pl.when(pl.program_id(1) == 0)
    def _():
        h_ref[...] = _rms_bf16(x_ref[...], g_ref[...])

    o_ref[...] = _dot(h_ref[...], w_ref[...].astype(BF16)).astype(o_ref.dtype)


def norm_matmul(x, g3, w3, layer, *, tm, tn):
    m, k = x.shape
    n = w3.shape[-1]
    return pl.pallas_call(
        _norm_matmul_kernel,
        grid=(m // tm, n // tn),
        in_specs=[
            pl.BlockSpec((tm, k), lambda i, j: (i, 0), pipeline_mode=pl.Buffered(1)),
            pl.BlockSpec((None, 1, k), lambda i, j: (layer, 0, 0)),
            pl.BlockSpec((None, k, tn), lambda i, j: (layer, 0, j)),
        ],
        out_specs=pl.BlockSpec((tm, tn), lambda i, j: (i, j)),
        out_shape=jax.ShapeDtypeStruct((m, n), F32),
        scratch_shapes=[pltpu.VMEM((tm, k), BF16)],
        compiler_params=_params("arbitrary", "arbitrary"),
        name="norm_matmul",
    )(x, g3, w3)


def _in_proj_kernel(n_heads, dh, first, x_ref, g_ref, w_ref, *refs):
    p_ref, k_ref, v_ref, h_ref = refs if first else refs[2:]
    tm, tn = p_ref.shape
    j = pl.program_id(1)

    @pl.when(j == 0)
    def _():
        h_ref[...] = _rms_bf16(x_ref[...], g_ref[...])

    o = _dot(h_ref[...], w_ref[...].astype(BF16))
    p_ref[...] = o
    heads_per_step = tn // dh
    group = n_heads // heads_per_step
    for s in range(group):
        for dst_ref, step in ((k_ref, group + s), (v_ref, 2 * group + s)):
            @pl.when(j == step)
            def _(dst_ref=dst_ref):
                for hh in range(heads_per_step):
                    dst_ref[pl.ds(s * heads_per_step + hh, tm, stride=n_heads), :] = o[:, hh * dh:(hh + 1) * dh]


def in_proj_prompt(x, g3, w3, layer, n_heads, dh, kv_prev, *, tm, tn):
    m, k = x.shape
    depth, _, n = w3.shape
    first = kv_prev is None
    assert (n_heads * dh) % tn == 0 and tn % dh == 0
    kv_shape = jax.ShapeDtypeStruct((depth, m * n_heads, dh), F32)
    kv_spec = pl.BlockSpec((None, tm * n_heads, dh), lambda i, j: (layer, i, 0))
    in_specs = [
        pl.BlockSpec((tm, k), lambda i, j: (i, 0), pipeline_mode=pl.Buffered(1)),
        pl.BlockSpec((None, 1, k), lambda i, j: (layer, 0, 0)),
        pl.BlockSpec((None, k, tn), lambda i, j: (layer, 0, j)),
    ]
    args = [x, g3, w3]
    aliases = {}
    if not first:
        in_specs += [pl.BlockSpec(memory_space=pl.ANY)] * 2
        args += list(kv_prev)
        aliases = {3: 1, 4: 2}
    return pl.pallas_call(
        functools.partial(_in_proj_kernel, n_heads, dh, first),
        grid=(m // tm, n // tn),
        in_specs=in_specs,
        out_specs=[pl.BlockSpec((tm, tn), lambda i, j: (i, j)), kv_spec, kv_spec],
        out_shape=[jax.ShapeDtypeStruct((m, n), F32), kv_shape, kv_shape],
        scratch_shapes=[pltpu.VMEM((tm, k), BF16)],
        input_output_aliases=aliases,
        compiler_params=_params("arbitrary", "arbitrary"),
        name="in_proj_prompt",
    )(*args)


def _matmul_residual_kernel(n_pieces, *refs):
    a_refs = refs[:n_pieces]
    w_refs = refs[n_pieces:2 * n_pieces]
    r_ref, o_ref = refs[2 * n_pieces:]
    acc = r_ref[...]
    for a_ref, w_ref in zip(a_refs, w_refs):
        acc = acc + _dot(a_ref[...].astype(BF16), w_ref[...].astype(BF16))
    o_ref[...] = acc


def matmul_residual(pieces, w3, layer, res, *, tm, tn):
    m = res.shape[0]
    n = w3.shape[-1]
    kp = pieces[0].shape[-1]
    n_pieces = len(pieces)
    in_specs = [pl.BlockSpec((tm, kp), lambda i, j: (i, 0)) for _ in pieces]
    in_specs += [pl.BlockSpec((None, kp, tn), functools.partial(lambda p, i, j: (layer, p, j), p))
                 for p in range(n_pieces)]
    in_specs += [pl.BlockSpec((tm, tn), lambda i, j: (i, j))]
    return pl.pallas_call(
        functools.partial(_matmul_residual_kernel, n_pieces),
        grid=(m // tm, n // tn),
        in_specs=in_specs,
        out_specs=pl.BlockSpec((tm, tn), lambda i, j: (i, j)),
        out_shape=jax.ShapeDtypeStruct((m, n), F32),
        compiler_params=_params("arbitrary", "arbitrary"),
        name="matmul_residual",
    )(*pieces, *([w3] * n_pieces), res)


def _ffn_kernel(final_norm, x_ref, g_ref, wg_ref, wu_ref, wd_ref, *refs):
    gf_ref = refs[0] if final_norm else None
    o_ref, h_ref = refs[-2:]

    @pl.when(pl.program_id(1) == 0)
    def _():
        x = x_ref[...]
        h_ref[...] = _rms_bf16(x, g_ref[...])
        o_ref[...] = x

    h = h_ref[...]
    gate = _dot(h, wg_ref[...].astype(BF16))
    up = _dot(h, wu_ref[...].astype(BF16))
    act = (gate * _sigmoid(gate) * up).astype(BF16)
    o_ref[...] += _dot(act, wd_ref[...].astype(BF16))

    if final_norm:
        @pl.when(pl.program_id(1) == pl.num_programs(1) - 1)
        def _():
            o_ref[...] = _rms(o_ref[...], gf_ref[...])


def ffn_residual(x, g3, wg3, wu3, wd3, layer, *, tm, tf, g_final2=None):
    m, d = x.shape
    f = wg3.shape[-1]
    final_norm = g_final2 is not None
    in_specs = [
        pl.BlockSpec((tm, d), lambda i, j: (i, 0), pipeline_mode=pl.Buffered(1)),
        pl.BlockSpec((None, 1, d), lambda i, j: (layer, 0, 0)),
        pl.BlockSpec((None, d, tf), lambda i, j: (layer, 0, j)),
        pl.BlockSpec((None, d, tf), lambda i, j: (layer, 0, j)),
        pl.BlockSpec((None, tf, d), lambda i, j: (layer, j, 0)),
    ]
    args = [x, g3, wg3, wu3, wd3]
    if final_norm:
        in_specs.append(pl.BlockSpec((1, d), lambda i, j: (0, 0)))
        args.append(g_final2)
    return pl.pallas_call(
        functools.partial(_ffn_kernel, final_norm),
        grid=(m // tm, f // tf),
        in_specs=in_specs,
        out_specs=pl.BlockSpec((tm, d), lambda i, j: (i, 0)),
        out_shape=jax.ShapeDtypeStruct((m, d), F32),
        scratch_shapes=[pltpu.VMEM((tm, d), BF16)],
        compiler_params=_params("arbitrary", "arbitrary"),
        name="ffn_residual",
    )(*args)


def _cross_shared_kernel(scale, x_ref, g_ref, wq_ref, mk_ref, mv_ref, wo_ref, o_ref, h_ref):
    @pl.when(pl.program_id(1) == 0)
    def _():
        x = x_ref[...]
        h_ref[...] = _rms_bf16(x, g_ref[...])
        o_ref[...] = x

    q = _dot(h_ref[...], wq_ref[...].astype(BF16))
    s = _dot_nt(q.astype(BF16), mk_ref[...].astype(BF16)) * scale
    e = jnp.exp(s - jnp.max(s, axis=-1, keepdims=True))
    p = e / jnp.sum(e, axis=-1, keepdims=True)
    o = _dot(p.astype(BF16), mv_ref[...].astype(BF16))
    o_ref[...] += _dot(o.astype(BF16), wo_ref[...].astype(BF16))


def cross_shared_residual(x, g3, wq3, mk, mv, wo3, layer, n_heads, *, tm):
    m, d = x.shape
    hd = d // n_heads
    n_mem = mk.shape[0]
    return pl.pallas_call(
        functools.partial(_cross_shared_kernel, hd ** -0.5),
        grid=(m // tm, n_heads),
        in_specs=[
            pl.BlockSpec((tm, d), lambda i, j: (i, 0), pipeline_mode=pl.Buffered(1)),
            pl.BlockSpec((None, 1, d), lambda i, j: (layer, 0, 0)),
            pl.BlockSpec((None, d, hd), lambda i, j: (layer, 0, j)),
            pl.BlockSpec((n_mem, hd), lambda i, j: (0, j)),
            pl.BlockSpec((n_mem, hd), lambda i, j: (0, j)),
            pl.BlockSpec((None, hd, d), lambda i, j: (layer, j, 0)),
        ],
        out_specs=pl.BlockSpec((tm, d), lambda i, j: (i, 0)),
        out_shape=jax.ShapeDtypeStruct((m, d), F32),
        scratch_shapes=[pltpu.VMEM((tm, d), BF16)],
        compiler_params=_params("arbitrary", "arbitrary"),
        name="cross_shared_residual",
    )(x, g3, wq3, mk, mv, wo3)


def _upper_ones(n):
    r = lax.broadcasted_iota(jnp.int32, (n, n), 0)
    c = lax.broadcasted_iota(jnp.int32, (n, n), 1)
    return jnp.where(c > r, 1.0, 0.0).astype(BF16)


BIAS_ROWS = 16
LOG2E = 1.4426950408889634


def _softplus2(u):
    return jnp.maximum(u, 0.0) + jnp.log(1.0 + jnp.exp2(_neg_abs(u))) * LOG2E


def _sb_prompt_kernel(layer, n_chunks, b_ref, q_ref, k_ref, v_ref, o_ref, kb_ref, vt_ref, acc_ref, carry_ref):
    h = pl.program_id(0)
    i = pl.program_id(1)
    tq, dh = q_ref.shape
    ck = SB_KEY_CHUNK
    per_q = tq // ck

    @pl.when(i == 0)
    def _():
        b2 = jnp.full((ck, BIAS_ROWS), b_ref[layer, h] * LOG2E, F32)
        col = lax.broadcasted_iota(jnp.int32, (ck, BIAS_ROWS), 1)
        b_hi = b2.astype(BF16).astype(F32)
        b_cols = jnp.where(col == 0, b_hi, jnp.where(col == 1, b2 - b_hi, 0.0)).astype(BF16)

        def prep(c, _):
            r = pl.multiple_of(c * ck, ck)
            kb_ref[c, :, 0:dh] = k_ref[pl.ds(r, ck), :].astype(BF16)
            kb_ref[c, :, dh:] = b_cols
            vt_ref[c] = v_ref[pl.ds(r, ck), :].T.astype(BF16)
            return 0
        lax.fori_loop(0, n_chunks, prep, 0)

    ones = jnp.where(lax.broadcasted_iota(jnp.int32, (BIAS_ROWS, tq), 0) < 2, 1.0, 0.0)
    q_t = jnp.concatenate([(q_ref[...] * (dh ** -0.5 * LOG2E)).T, ones], axis=0).astype(BF16)
    tri = _upper_ones(ck)
    order = list(reversed(range(per_q)))

    zss, spbs, sp0s, keeps = [], [], [], []
    for d in order:
        u = _dot(kb_ref[i * per_q + d], q_t[:, d * ck:])
        keep = lax.broadcasted_iota(jnp.int32, u.shape, 0) < lax.broadcasted_iota(jnp.int32, u.shape, 1)
        sp = _softplus2(u)
        spm = jnp.where(keep, sp, 0.0)
        zss.append(u - sp)
        spbs.append(spm.astype(BF16))
        sp0s.append(spm[0:1, :])
        keeps.append(keep)
    withins = [_dot(tri, spb) for spb in spbs]
    carry = jnp.zeros((1, tq), F32)
    acc_ref[...] = jnp.zeros_like(acc_ref)
    for d, zs, keep, sp0, within in zip(order, zss, keeps, sp0s, withins):
        lo = d * ck
        w = jnp.where(keep, jnp.exp2(zs - (within + carry[:, lo:])), 0.0)
        seen = carry[:, lo:] + within[0:1, :] + sp0
        carry = seen if lo == 0 else jnp.concatenate([carry[:, :lo], seen], axis=1)
        acc_ref[:, lo:] += _dot(vt_ref[i * per_q + d], w.astype(BF16))
    carry_ref[...] = carry

    def below(it, _):
        base = (i - 1 - it) * per_q
        zss, spbs, sp0s = [], [], []
        for d in order:
            u = _dot(kb_ref[base + d], q_t)
            sp = _softplus2(u)
            zss.append(u - sp)
            spbs.append(sp.astype(BF16))
            sp0s.append(sp[0:1, :])
        withins = [_dot(tri, spb) for spb in spbs]
        carry = carry_ref[...]
        pv = None
        for d, zs, sp0, within in zip(order, zss, sp0s, withins):
            w = jnp.exp2(zs - (within + carry))
            carry = carry + within[0:1, :] + sp0
            part = _dot(vt_ref[base + d], w.astype(BF16))
            pv = part if pv is None else pv + part
        carry_ref[...] = carry
        acc_ref[...] += pv
        return 0

    lax.fori_loop(0, i, below, 0)
    o_ref[...] = acc_ref[...].T.astype(o_ref.dtype)


def sb_prompt(p, b_sb, layer, n_heads, *, tq):
    s = p.shape[0]
    dh = SB_HEAD_DIM
    n_chunks = s // SB_KEY_CHUNK
    return pl.pallas_call(
        functools.partial(_sb_prompt_kernel, layer, n_chunks),
        grid=(n_heads, s // tq),
        in_specs=[
            pl.BlockSpec(memory_space=pltpu.SMEM),
            pl.BlockSpec((tq, dh), lambda h, i: (i, h)),
            pl.BlockSpec((s, dh), lambda h, i: (0, n_heads + h)),
            pl.BlockSpec((s, dh), lambda h, i: (0, 2 * n_heads + h)),
        ],
        out_specs=pl.BlockSpec((tq, dh), lambda h, i: (i, h)),
        out_shape=jax.ShapeDtypeStruct((s, n_heads * dh), BF16),
        scratch_shapes=[
            pltpu.VMEM((n_chunks, SB_KEY_CHUNK, dh + BIAS_ROWS), BF16),
            pltpu.VMEM((n_chunks, dh, SB_KEY_CHUNK), BF16),
            pltpu.VMEM((dh, tq), F32),
            pltpu.VMEM((1, tq), F32),
        ],
        compiler_params=_params("arbitrary", "arbitrary"),
        name="sb_prompt",
    )(b_sb, p, p, p)


DECODE_LANES = 128
DECODE_ROWS_PER_T = 8
DECODE_ROWS_SHIFT = 3
MASKED_SCORE = -1e30


def _log2(n):
    assert n & (n - 1) == 0
    return n.bit_length() - 1


def _block_diag_queries(q_ref, n_t, n_heads, scale):
    d = q_ref.shape[-1]
    hd_shift = _log2(d // n_heads)
    r = lax.broadcasted_iota(jnp.int32, (DECODE_LANES, d), 0)
    c = lax.broadcasted_iota(jnp.int32, (DECODE_LANES, d), 1)
    t_of_r = r >> DECODE_ROWS_SHIFT
    h_of_r = r & (DECODE_ROWS_PER_T - 1)
    out = jnp.zeros((DECODE_LANES, d), F32)
    for t in range(n_t):
        out = jnp.where(t_of_r == t, q_ref[t:t + 1, :] * scale, out)
    return jnp.where(((c >> hd_shift) == h_of_r) & (t_of_r < n_t), out, 0.0)


def _take_block_diag(full, o_ref, n_t, n_heads):
    d = full.shape[-1]
    hd_shift = _log2(d // n_heads)
    r = lax.broadcasted_iota(jnp.int32, (DECODE_ROWS_PER_T, d), 0)
    c = lax.broadcasted_iota(jnp.int32, (DECODE_ROWS_PER_T, d), 1)
    own = (c >> hd_shift) == r
    for t in range(n_t):
        rows = full[t * DECODE_ROWS_PER_T:(t + 1) * DECODE_ROWS_PER_T, :]
        o_ref[t:t + 1, :] = jnp.sum(jnp.where(own, rows, 0.0), axis=0, keepdims=True).astype(o_ref.dtype)


NEW_KEY_ROWS = 16


def _sb_sample_kernel(n_pages, n_heads, pt_ref, bl_ref, q_ref, kn_ref, vn_ref, *refs):
    k_pages = refs[:n_pages]
    v_pages = refs[n_pages:2 * n_pages]
    o_ref, kf_ref, vf_ref, kx_ref, vx_ref = refs[2 * n_pages:]
    n_t, d = q_ref.shape
    dh = d // n_heads
    ck = SB_KEY_CHUNK
    page = k_pages[0].shape[0] // n_heads
    n_past = n_pages * page // ck
    new0 = n_past * ck

    @pl.when(pl.program_id(0) == 0)
    def _():
        kx_ref[...] = jnp.zeros_like(kx_ref)
        vx_ref[...] = jnp.zeros_like(vx_ref)
        kf_ref[new0:, :] = jnp.zeros((ck, d), BF16)
        vf_ref[new0:, :] = jnp.zeros((ck, d), BF16)

    def relayout(src_refs, dst_ref):
        for pg, src in enumerate(src_refs):
            for h in range(n_heads):
                dst_ref[pg * page:(pg + 1) * page, h * dh:(h + 1) * dh] = (
                    src[pl.ds(h, page, stride=n_heads), :].astype(BF16))

    kx_ref[0:n_t, :] = kn_ref[...]
    kf_ref[new0:new0 + NEW_KEY_ROWS, :] = kx_ref[...].astype(BF16)
    vx_ref[0:n_t, :] = vn_ref[...]
    vf_ref[new0:new0 + NEW_KEY_ROWS, :] = vx_ref[...].astype(BF16)
    relayout(k_pages, kf_ref)
    relayout(v_pages, vf_ref)
    qbd_t = _block_diag_queries(q_ref, n_t, n_heads, dh ** -0.5).T.astype(BF16)
    zt = _dot(kf_ref[...], qbd_t) + bl_ref[...]

    rows = lax.broadcasted_iota(jnp.int32, zt.shape, 0)
    t_of_lane = lax.broadcasted_iota(jnp.int32, zt.shape, 1) >> DECODE_ROWS_SHIFT
    keep = rows - new0 < t_of_lane
    sp = _softplus(zt)
    spm = jnp.where(keep, sp, 0.0)
    zs = zt - sp
    tri = _upper_ones(ck)
    within = [_dot(tri, spm[c * ck:(c + 1) * ck].astype(BF16)) for c in range(n_past + 1)]
    carry = jnp.zeros((1, DECODE_LANES), F32)
    pieces = [None] * (n_past + 1)
    for c in reversed(range(n_past + 1)):
        lo = c * ck
        w = jnp.exp(zs[lo:lo + ck] - (within[c] + carry))
        if c == n_past:
            w = jnp.where(keep[lo:lo + ck], w, 0.0)
        pieces[c] = w.T.astype(BF16)
        carry = carry + within[c][0:1, :] + spm[lo:lo + 1, :]
    full = _dot(jnp.concatenate(pieces, axis=1), vf_ref[...])
    _take_block_diag(full, o_ref, n_t, n_heads)


def sb_sample(p3, cache_k, cache_v, page_table, bias_lanes, layer, n_heads):
    db, n_t, _ = p3.shape
    n_pages = page_table.shape[1]
    page_rows, dh = cache_k.shape[2:]
    page = page_rows // n_heads
    d = n_heads * dh
    assert (n_pages * page) % SB_KEY_CHUNK == 0 and SB_KEY_CHUNK % page == 0
    assert n_t * DECODE_ROWS_PER_T <= DECODE_LANES and n_heads <= DECODE_ROWS_PER_T and n_t <= NEW_KEY_ROWS
    n_chunks = n_pages * page // SB_KEY_CHUNK

    def page_spec(pg):
        return pl.BlockSpec((None, None, page_rows, dh), lambda b, pt: (layer, pt[b, pg], 0, 0))

    grid_spec = pltpu.PrefetchScalarGridSpec(
        num_scalar_prefetch=1,
        grid=(db,),
        in_specs=[
            pl.BlockSpec((None, 1, DECODE_LANES), lambda b, pt: (layer, 0, 0)),
            pl.BlockSpec((None, n_t, d), lambda b, pt: (b, 0, 0)),
            pl.BlockSpec((None, n_t, d), lambda b, pt: (b, 0, 1)),
            pl.BlockSpec((None, n_t, d), lambda b, pt: (b, 0, 2)),
        ] + [page_spec(pg) for pg in range(n_pages)] * 2,
        out_specs=pl.BlockSpec((None, n_t, d), lambda b, pt: (b, 0, 0)),
        scratch_shapes=[
            pltpu.VMEM(((n_chunks + 1) * SB_KEY_CHUNK, d), BF16),
            pltpu.VMEM(((n_chunks + 1) * SB_KEY_CHUNK, d), BF16),
            pltpu.VMEM((NEW_KEY_ROWS, d), F32),
            pltpu.VMEM((NEW_KEY_ROWS, d), F32),
        ],
    )
    return pl.pallas_call(
        functools.partial(_sb_sample_kernel, n_pages, n_heads),
        grid_spec=grid_spec,
        out_shape=jax.ShapeDtypeStruct((db, n_t, d), F32),
        compiler_params=_params("arbitrary"),
        name="sb_sample",
    )(page_table, bias_lanes, p3, p3, p3, *([cache_k] * n_pages), *([cache_v] * n_pages))


def _cross_sample_kernel(q_ref, mk_ref, mv_ref, o_ref):
    bb, n_t, d = q_ref.shape
    n_mem, n_heads, hd = mk_ref.shape[1:]
    n_rows = n_mem * n_heads
    q_rows = n_t * DECODE_ROWS_PER_T
    r = lax.broadcasted_iota(jnp.int32, (q_rows, hd), 0)
    row_h = lax.broadcasted_iota(jnp.int32, (n_rows, DECODE_LANES), 0) & (n_heads - 1)
    lane_h = lax.broadcasted_iota(jnp.int32, (n_rows, DECODE_LANES), 1) & (DECODE_ROWS_PER_T - 1)
    own_head = row_h == lane_h
    for b in range(bb):
        qcat = jnp.zeros((q_rows, hd), F32)
        for t in range(n_t):
            for h in range(n_heads):
                qcat = jnp.where(r == t * DECODE_ROWS_PER_T + h, q_ref[b, t:t + 1, h * hd:(h + 1) * hd], qcat)
        qcat = jnp.concatenate([qcat * hd ** -0.5, jnp.zeros((DECODE_LANES - q_rows, hd), F32)], axis=0)
        mk2 = mk_ref[b].reshape(n_rows, hd).astype(BF16)
        mv2 = mv_ref[b].reshape(n_rows, hd).astype(BF16)
        st = jnp.where(own_head, _dot_nt(mk2, qcat.astype(BF16)), MASKED_SCORE)
        e = jnp.exp(st - jnp.max(st, axis=0, keepdims=True))
        p = e / jnp.sum(e, axis=0, keepdims=True)
        out = _dot(p.T.astype(BF16), mv2)
        for t in range(n_t):
            for h in range(n_heads):
                row = t * DECODE_ROWS_PER_T + h
                o_ref[b, t:t + 1, h * hd:(h + 1) * hd] = out[row:row + 1, :]


def cross_sample(q3, mem_k, mem_v, layer, *, bb):
    db, n_t, d = q3.shape
    n_mem, n_heads, hd = mem_k.shape[2:]
    assert n_t * DECODE_ROWS_PER_T <= DECODE_LANES and n_heads <= DECODE_ROWS_PER_T
    assert n_heads & (n_heads - 1) == 0 and db % bb == 0
    mem_spec = pl.BlockSpec((None, bb, n_mem, n_heads, hd), lambda b: (layer, b, 0, 0, 0))
    return pl.pallas_call(
        _cross_sample_kernel,
        grid=(db // bb,),
        in_specs=[pl.BlockSpec((bb, n_t, d), lambda b: (b, 0, 0)), mem_spec, mem_spec],
        out_specs=pl.BlockSpec((bb, n_t, d), lambda b: (b, 0, 0)),
        out_shape=jax.ShapeDtypeStruct((db, n_t, d), F32),
        compiler_params=_params("arbitrary"),
        name="cross_sample",
    )(q3, mem_k, mem_v)


def _ln_swish(c, g, b):
    mu = jnp.mean(c, axis=-1, keepdims=True)
    xc = c - mu
    var = jnp.mean(xc * xc, axis=-1, keepdims=True)
    y = xc * lax.rsqrt(var + EPS) * g + b
    return y * _sigmoid(y)


def _conv_prompt_kernel(width, halo, a_ref, gt_ref, ah_ref, gh_ref, w_ref, bd_ref, g_ref, b_ref,
                        o_ref, tail_ref, u_ref):
    i = pl.program_id(0)
    tb = a_ref.shape[0]
    u_halo = ah_ref[...] * _sigmoid(gh_ref[...])
    u_ref[0, 0:halo, :] = jnp.where(i == 0, 0.0, u_halo)
    u_ref[0, halo:halo + tb, :] = a_ref[...] * _sigmoid(gt_ref[...])
    n_rows = halo + tb - SUBLANES
    for s in range(1, SUBLANES):
        u_ref[s, 0:n_rows, :] = u_ref[0, s:s + n_rows, :]
    first = halo - (width - 1)
    rc = CONV_ROW_CHUNK

    for r0 in range(0, tb, rc):
        acc = jnp.broadcast_to(bd_ref[...], (rc, bd_ref.shape[-1]))
        for w in range(width):
            s, base = (first + w) % SUBLANES, (first + w) // SUBLANES * SUBLANES
            acc = acc + u_ref[s, r0 + base:r0 + base + rc, :] * w_ref[w:w + 1, :]
        o_ref[r0:r0 + rc, :] = _ln_swish(acc, g_ref[...], b_ref[...]).astype(o_ref.dtype)

    @pl.when(i == pl.num_programs(0) - 1)
    def _():
        tail_ref[...] = u_ref[0, tb:tb + halo, :]


def conv_prompt(p, w_dw, b_dw3, g3, b3, layer, *, tb):
    s = p.shape[0]
    width, c = w_dw.shape[1:]
    halo = 32
    assert width - 1 <= halo and tb % halo == 0
    hb = tb // halo
    vec = pl.BlockSpec((None, 1, c), lambda i: (layer, 0, 0))
    return pl.pallas_call(
        functools.partial(_conv_prompt_kernel, width, halo),
        grid=(s // tb,),
        in_specs=[
            pl.BlockSpec((tb, c), lambda i: (i, 3)),
            pl.BlockSpec((tb, c), lambda i: (i, 4)),
            pl.BlockSpec((halo, c), lambda i: (jnp.maximum(i * hb - 1, 0), 3)),
            pl.BlockSpec((halo, c), lambda i: (jnp.maximum(i * hb - 1, 0), 4)),
            pl.BlockSpec((None, width, c), lambda i: (layer, 0, 0)),
            vec, vec, vec,
        ],
        out_specs=[pl.BlockSpec((tb, c), lambda i: (i, 0)), pl.BlockSpec((halo, c), lambda i: (0, 0))],
        out_shape=[jax.ShapeDtypeStruct((s, c), BF16), jax.ShapeDtypeStruct((halo, c), F32)],
        scratch_shapes=[pltpu.VMEM((SUBLANES, tb + halo, c), F32)],
        compiler_params=_params("arbitrary"),
        name="conv_prompt",
    )(p, p, p, p, w_dw, b_dw3, g3, b3)


def _conv_sample_kernel(width, a_ref, gt_ref, hist_ref, w_ref, bd_ref, g_ref, b_ref,
                        o_ref, hist_out_ref, u_ref, w_pad_ref):
    bb, n_t, c = a_ref.shape
    n_hist = width - 1
    taps = w_pad_ref.shape[0]
    w_pad_ref[0:width, :] = w_ref[...]
    w_pad_ref[width:taps, :] = jnp.zeros((taps - width, c), F32)
    u_ref[n_hist + n_t:, :] = jnp.zeros((u_ref.shape[0] - n_hist - n_t, c), F32)
    for b in range(bb):
        u_ref[0:n_hist, :] = hist_ref[b]
        u_ref[n_hist:n_hist + n_t, :] = a_ref[b] * _sigmoid(gt_ref[b])
        for t in range(n_t):
            conv = jnp.sum(u_ref[t:t + taps, :] * w_pad_ref[...], axis=0, keepdims=True) + bd_ref[...]
            o_ref[b, t:t + 1, :] = _ln_swish(conv, g_ref[...], b_ref[...])
        hist_out_ref[b] = u_ref[n_t:n_t + n_hist, :]


def conv_sample(p3, hist, w_dw, b_dw3, g3, b3, layer, *, bb):
    db, n_t, _ = p3.shape
    width, c = w_dw.shape[1:]
    n_hist = width - 1
    taps = 32
    assert width <= taps
    vec = pl.BlockSpec((None, 1, c), lambda i: (layer, 0, 0))
    return pl.pallas_call(
        functools.partial(_conv_sample_kernel, width),
        grid=(db // bb,),
        in_specs=[
            pl.BlockSpec((bb, n_t, c), lambda i: (i, 0, 3)),
            pl.BlockSpec((bb, n_t, c), lambda i: (i, 0, 4)),
            pl.BlockSpec((None, bb, n_hist, c), lambda i: (layer, i, 0, 0)),
            pl.BlockSpec((None, width, c), lambda i: (layer, 0, 0)),
            vec, vec, vec,
        ],
        out_specs=[pl.BlockSpec((bb, n_t, c), lambda i: (i, 0, 0)),
                   pl.BlockSpec((bb, n_hist, c), lambda i: (i, 0, 0))],
        out_shape=[jax.ShapeDtypeStruct((db, n_t, c), F32), jax.ShapeDtypeStruct((db, n_hist, c), F32)],
        scratch_shapes=[pltpu.VMEM((n_t + taps + 4, c), F32), pltpu.VMEM((taps, c), F32)],
        compiler_params=_params("arbitrary"),
        name="conv_sample",
    )(p3, p3, hist, w_dw, b_dw3, g3, b3)


def kernel(x_prompt, x_sample, cache_sb_k, cache_sb_v, state_conv, cache_mem_k, cache_mem_v, page_table, mem_prompt, g_mix, w_in, b_sb, w_dw, b_dw, g_cln, b_cln, w_out, g_cross, g_mem, w_xq, w_xk, w_xv, w_xo, g_ffn, w_gate, w_up, w_down, g_final):
    batch, seq, d = x_prompt.shape
    db, n_t, _ = x_sample.shape
    depth = w_in.shape[0]
    n_sb_heads = b_sb.shape[1]
    sb_dim = n_sb_heads * SB_HEAD_DIM
    conv_dim = w_dw.shape[2]
    n_hist = w_dw.shape[1] - 1
    n_mem, n_x_heads, x_hd = cache_mem_k.shape[2:]
    n_pool, page = cache_sb_k.shape[1:3]
    assert batch == 1 and w_in.shape[2] == 3 * sb_dim + 2 * conv_dim and sb_dim == conv_dim
    ms = db * n_t

    col = lambda a: a.reshape(depth, 1, a.shape[-1])
    g_mix3, g_cross3, g_mem3, g_ffn3 = col(g_mix), col(g_cross), col(g_mem), col(g_ffn)
    b_dw3, g_cln3, b_cln3 = col(b_dw), col(g_cln), col(b_cln)
    cache_k = cache_sb_k.reshape(depth, n_pool, page * n_sb_heads, SB_HEAD_DIM)
    cache_v = cache_sb_v.reshape(depth, n_pool, page * n_sb_heads, SB_HEAD_DIM)
    lane_h = jnp.arange(DECODE_LANES) % DECODE_ROWS_PER_T
    bias_lanes = jnp.where(lane_h < n_sb_heads, b_sb[:, jnp.minimum(lane_h, n_sb_heads - 1)], 0.0)
    bias_lanes = bias_lanes.reshape(depth, 1, DECODE_LANES).astype(F32)

    tm_p = 1024
    xp = x_prompt.reshape(seq, d)
    xs = x_sample.reshape(ms, d)
    mem = mem_prompt.reshape(n_mem, d)
    sks, svs, cvp, cvs, mkp, mvp = [], [], [], [], [], []
    kv_prompt = None
    for l in range(depth):
        p, *kv_prompt = in_proj_prompt(xp, g_mix3, w_in, l, n_sb_heads, SB_HEAD_DIM, kv_prompt, tm=tm_p, tn=512)
        o_sb = sb_prompt(p, b_sb, l, n_sb_heads, tq=PROMPT_Q_BLOCK)
        c, tail = conv_prompt(p, w_dw, b_dw3, g_cln3, b_cln3, l, tb=256)
        xp = matmul_residual([o_sb, c], w_out, l, xp, tm=2 * tm_p, tn=512)
        cvp.append(tail[tail.shape[0] - n_hist:])

        ps = norm_matmul(xs, g_mix3, w_in, l, tm=ms, tn=512)
        ps3 = ps.reshape(db, n_t, ps.shape[-1])
        o_sb = sb_sample(ps3, cache_k, cache_v, page_table, bias_lanes, l, n_sb_heads)
        c, hist = conv_sample(ps3, state_conv, w_dw, b_dw3, g_cln3, b_cln3, l, bb=8)
        xs = matmul_residual([o_sb.reshape(ms, sb_dim), c.reshape(ms, conv_dim)], w_out, l, xs, tm=ms, tn=512)
        sks.append(ps[:, sb_dim:2 * sb_dim])
        svs.append(ps[:, 2 * sb_dim:3 * sb_dim])
        cvs.append(hist)

        mk = norm_matmul(mem, g_mem3, w_xk, l, tm=n_mem, tn=512)
        mv = norm_matmul(mem, g_mem3, w_xv, l, tm=n_mem, tn=512)
        mkp.append(mk)
        mvp.append(mv)
        xp = cross_shared_residual(xp, g_cross3, w_xq, mk, mv, w_xo, l, n_x_heads, tm=tm_p)
        qs = norm_matmul(xs, g_cross3, w_xq, l, tm=ms, tn=512)
        o_x = cross_sample(qs.reshape(db, n_t, d), cache_mem_k, cache_mem_v, l, bb=4)
        xs = matmul_residual([o_x.reshape(ms, d)], w_xo, l, xs, tm=ms, tn=512)

        g_last = g_final.reshape(1, d) if l == depth - 1 else None
        xp = ffn_residual(xp, g_ffn3, w_gate, w_up, w_down, l, tm=tm_p, tf=256, g_final2=g_last)
        xs = ffn_residual(xs, g_ffn3, w_gate, w_up, w_down, l, tm=ms, tf=256, g_final2=g_last)

    y_prompt = xp.reshape(batch, seq, d)
    y_sample = xs.reshape(db, n_t, d)
    sb_shape = lambda rows: (depth,) + rows + (n_sb_heads, SB_HEAD_DIM)
    return (
        y_prompt,
        y_sample,
        kv_prompt[0].reshape(sb_shape((batch, seq))),
        kv_prompt[1].reshape(sb_shape((batch, seq))),
        jnp.stack(sks).reshape(sb_shape((db, n_t))),
        jnp.stack(svs).reshape(sb_shape((db, n_t))),
        jnp.stack(cvp).reshape(depth, batch, n_hist, conv_dim),
        jnp.stack(cvs),
        jnp.stack(mkp).reshape(depth, batch, n_mem, n_x_heads, x_hd),
        jnp.stack(mvp).reshape(depth, batch, n_mem, n_x_heads, x_hd),
    )
```

```python
import functools

import jax
import jax.numpy as jnp
from jax import lax
from jax.experimental import pallas as pl
from jax.experimental.pallas import tpu as pltpu

F32 = jnp.float32
BF16 = jnp.bfloat16
EPS = 1e-6

V7X_VMEM_LIMIT_BYTES = 56 * 1024 * 1024
SUBLANES = 8
SB_HEAD_DIM = 128
SB_KEY_CHUNK = 256
PROMPT_Q_BLOCK = 1024
CONV_ROW_CHUNK = 32


def _params(*semantics):
    return pltpu.CompilerParams(dimension_semantics=semantics, vmem_limit_bytes=V7X_VMEM_LIMIT_BYTES)


def _rms(x, g):
    return x * lax.rsqrt(jnp.mean(x * x, axis=-1, keepdims=True) + EPS) * g


def _rms_bf16(x, g):
    return _rms(x, g).astype(BF16)


def _sigmoid(x):
    return 1.0 / (1.0 + jnp.exp(-x))


def _neg_abs(z):
    sign = jnp.uint32(0x80000000)
    return lax.bitcast_convert_type(lax.bitcast_convert_type(z, jnp.uint32) | sign, F32)


def _softplus(z):
    return jnp.maximum(z, 0.0) + jnp.log(1.0 + jnp.exp(_neg_abs(z)))


def _dot(a, b):
    return jnp.dot(a, b, preferred_element_type=F32)


def _dot_nt(a, b):
    return lax.dot_general(a, b, (((1,), (1,)), ((), ())), preferred_element_type=F32)


def _norm_matmul_kernel(x_ref, g_ref, w_ref, o_ref, h_ref):
    @pl.when(pl.program_id(1) == 0)
    def _():
        h_ref[...] = _rms_bf16(x_ref[...], g_ref[...])

    o_ref[...] = _dot(h_ref[...], w_ref[...].astype(BF16)).astype(o_ref.dtype)


def norm_matmul(x, g3, w3, layer, *, tm, tn):
    m, k = x.shape
    n = w3.shape[-1]
    return pl.pallas_call(
        _norm_matmul_kernel,
        grid=(m // tm, n // tn),
        in_specs=[
            pl.BlockSpec((tm, k), lambda i, j: (i, 0), pipeline_mode=pl.Buffered(1)),
            pl.BlockSpec((None, 1, k), lambda i, j: (layer, 0, 0)),
            pl.BlockSpec((None, k, tn), lambda i, j: (layer, 0, j)),
        ],
        out_specs=pl.BlockSpec((tm, tn), lambda i, j: (i, j)),
        out_shape=jax.ShapeDtypeStruct((m, n), F32),
        scratch_shapes=[pltpu.VMEM((tm, k), BF16)],
        compiler_params=_params("arbitrary", "arbitrary"),
        name="norm_matmul",
    )(x, g3, w3)


def _in_proj_kernel(n_heads, dh, first, x_ref, g_ref, w_ref, *refs):
    p_ref, k_ref, v_ref, h_ref = refs if first else refs[2:]
    tm, tn = p_ref.shape
    j = pl.program_id(1)

    @pl.when(j == 0)
    def _():
        h_ref[...] = _rms_bf16(x_ref[...], g_ref[...])

    o = _dot(h_ref[...], w_ref[...].astype(BF16))
    p_ref[...] = o
    heads_per_step = tn // dh
    group = n_heads // heads_per_step
    for s in range(group):
        for dst_ref, step in ((k_ref, group + s), (v_ref, 2 * group + s)):
            @pl.when(j == step)
            def _(dst_ref=dst_ref):
                for hh in range(heads_per_step):
                    dst_ref[pl.ds(s * heads_per_step + hh, tm, stride=n_heads), :] = o[:, hh * dh:(hh + 1) * dh]


def in_proj_prompt(x, g3, w3, layer, n_heads, dh, kv_prev, *, tm, tn):
    m, k = x.shape
    depth, _, n = w3.shape
    first = kv_prev is None
    assert (n_heads * dh) % tn == 0 and tn % dh == 0
    kv_shape = jax.ShapeDtypeStruct((depth, m * n_heads, dh), F32)
    kv_spec = pl.BlockSpec((None, tm * n_heads, dh), lambda i, j: (layer, i, 0))
    in_specs = [
        pl.BlockSpec((tm, k), lambda i, j: (i, 0), pipeline_mode=pl.Buffered(1)),
        pl.BlockSpec((None, 1, k), lambda i, j: (layer, 0, 0)),
        pl.BlockSpec((None, k, tn), lambda i, j: (layer, 0, j)),
    ]
    args = [x, g3, w3]
    aliases = {}
    if not first:
        in_specs += [pl.BlockSpec(memory_space=pl.ANY)] * 2
        args += list(kv_prev)
        aliases = {3: 1, 4: 2}
    return pl.pallas_call(
        functools.partial(_in_proj_kernel, n_heads, dh, first),
        grid=(m // tm, n // tn),
        in_specs=in_specs,
        out_specs=[pl.BlockSpec((tm, tn), lambda i, j: (i, j)), kv_spec, kv_spec],
        out_shape=[jax.ShapeDtypeStruct((m, n), F32), kv_shape, kv_shape],
        scratch_shapes=[pltpu.VMEM((tm, k), BF16)],
        input_output_aliases=aliases,
        compiler_params=_params("arbitrary", "arbitrary"),
        name="in_proj_prompt",
    )(*args)


def _matmul_residual_kernel(n_pieces, *refs):
    a_refs = refs[:n_pieces]
    w_refs = refs[n_pieces:2 * n_pieces]
    r_ref, o_ref = refs[2 * n_pieces:]
    acc = r_ref[...]
    for a_ref, w_ref in zip(a_refs, w_refs):
        acc = acc + _dot(a_ref[...].astype(BF16), w_ref[...].astype(BF16))
    o_ref[...] = acc


def matmul_residual(pieces, w3, layer, res, *, tm, tn):
    m = res.shape[0]
    n = w3.shape[-1]
    kp = pieces[0].shape[-1]
    n_pieces = len(pieces)
    in_specs = [pl.BlockSpec((tm, kp), lambda i, j: (i, 0)) for _ in pieces]
    in_specs += [pl.BlockSpec((None, kp, tn), functools.partial(lambda p, i, j: (layer, p, j), p))
                 for p in range(n_pieces)]
    in_specs += [pl.BlockSpec((tm, tn), lambda i, j: (i, j))]
    return pl.pallas_call(
        functools.partial(_matmul_residual_kernel, n_pieces),
        grid=(m // tm, n // tn),
        in_specs=in_specs,
        out_specs=pl.BlockSpec((tm, tn), lambda i, j: (i, j)),
        out_shape=jax.ShapeDtypeStruct((m, n), F32),
        compiler_params=_params("arbitrary", "arbitrary"),
        name="matmul_residual",
    )(*pieces, *([w3] * n_pieces), res)


def _ffn_kernel(final_norm, x_ref, g_ref, wg_ref, wu_ref, wd_ref, *refs):
    gf_ref = refs[0] if final_norm else None
    o_ref, h_ref = refs[-2:]

    @pl.when(pl.program_id(1) == 0)
    def _():
        x = x_ref[...]
        h_ref[...] = _rms_bf16(x, g_ref[...])
        o_ref[...] = x

    h = h_ref[...]
    gate = _dot(h, wg_ref[...].astype(BF16))
    up = _dot(h, wu_ref[...].astype(BF16))
    act = (gate * _sigmoid(gate) * up).astype(BF16)
    o_ref[...] += _dot(act, wd_ref[...].astype(BF16))

    if final_norm:
        @pl.when(pl.program_id(1) == pl.num_programs(1) - 1)
        def _():
            o_ref[...] = _rms(o_ref[...], gf_ref[...])


def ffn_residual(x, g3, wg3, wu3, wd3, layer, *, tm, tf, g_final2=None):
    m, d = x.shape
    f = wg3.shape[-1]
    final_norm = g_final2 is not None
    in_specs = [
        pl.BlockSpec((tm, d), lambda i, j: (i, 0), pipeline_mode=pl.Buffered(1)),
        pl.BlockSpec((None, 1, d), lambda i, j: (layer, 0, 0)),
        pl.BlockSpec((None, d, tf), lambda i, j: (layer, 0, j)),
        pl.BlockSpec((None, d, tf), lambda i, j: (layer, 0, j)),
        pl.BlockSpec((None, tf, d), lambda i, j: (layer, j, 0)),
    ]
    args = [x, g3, wg3, wu3, wd3]
    if final_norm:
        in_specs.append(pl.BlockSpec((1, d), lambda i, j: (0, 0)))
        args.append(g_final2)
    return pl.pallas_call(
        functools.partial(_ffn_kernel, final_norm),
        grid=(m // tm, f // tf),
        in_specs=in_specs,
        out_specs=pl.BlockSpec((tm, d), lambda i, j: (i, 0)),
        out_shape=jax.ShapeDtypeStruct((m, d), F32),
        scratch_shapes=[pltpu.VMEM((tm, d), BF16)],
        compiler_params=_params("arbitrary", "arbitrary"),
        name="ffn_residual",
    )(*args)


def _cross_shared_kernel(scale, x_ref, g_ref, wq_ref, mk_ref, mv_ref, wo_ref, o_ref, h_ref):
    @pl.when(pl.program_id(1) == 0)
    def _():
        x = x_ref[...]
        h_ref[...] = _rms_bf16(x, g_ref[...])
        o_ref[...] = x

    q = _dot(h_ref[...], wq_ref[...].astype(BF16))
    s = _dot_nt(q.astype(BF16), mk_ref[...].astype(BF16)) * scale
    e = jnp.exp(s - jnp.max(s, axis=-1, keepdims=True))
    p = e / jnp.sum(e, axis=-1, keepdims=True)
    o = _dot(p.astype(BF16), mv_ref[...].astype(BF16))
    o_ref[...] += _dot(o.astype(BF16), wo_ref[...].astype(BF16))


def cross_shared_residual(x, g3, wq3, mk, mv, wo3, layer, n_heads, *, tm):
    m, d = x.shape
    hd = d // n_heads
    n_mem = mk.shape[0]
    return pl.pallas_call(
        functools.partial(_cross_shared_kernel, hd ** -0.5),
        grid=(m // tm, n_heads),
        in_specs=[
            pl.BlockSpec((tm, d), lambda i, j: (i, 0), pipeline_mode=pl.Buffered(1)),
            pl.BlockSpec((None, 1, d), lambda i, j: (layer, 0, 0)),
            pl.BlockSpec((None, d, hd), lambda i, j: (layer, 0, j)),
            pl.BlockSpec((n_mem, hd), lambda i, j: (0, j)),
            pl.BlockSpec((n_mem, hd), lambda i, j: (0, j)),
            pl.BlockSpec((None, hd, d), lambda i, j: (layer, j, 0)),
        ],
        out_specs=pl.BlockSpec((tm, d), lambda i, j: (i, 0)),
        out_shape=jax.ShapeDtypeStruct((m, d), F32),
        scratch_shapes=[pltpu.VMEM((tm, d), BF16)],
        compiler_params=_params("arbitrary", "arbitrary"),
        name="cross_shared_residual",
    )(x, g3, wq3, mk, mv, wo3)


def _upper_ones(n):
    r = lax.broadcasted_iota(jnp.int32, (n, n), 0)
    c = lax.broadcasted_iota(jnp.int32, (n, n), 1)
    return jnp.where(c > r, 1.0, 0.0).astype(BF16)


BIAS_ROWS = 16
LOG2E = 1.4426950408889634


def _softplus2(u):
    return jnp.maximum(u, 0.0) + jnp.log(1.0 + jnp.exp2(_neg_abs(u))) * LOG2E


def _sb_prompt_kernel(layer, n_chunks, b_ref, q_ref, k_ref, v_ref, o_ref, kb_ref, vt_ref, acc_ref, carry_ref):
    h = pl.program_id(0)
    i = pl.program_id(1)
    tq, dh = q_ref.shape
    ck = SB_KEY_CHUNK
    per_q = tq // ck

    @pl.when(i == 0)
    def _():
        b2 = jnp.full((ck, BIAS_ROWS), b_ref[layer, h] * LOG2E, F32)
        col = lax.broadcasted_iota(jnp.int32, (ck, BIAS_ROWS), 1)
        b_hi = b2.astype(BF16).astype(F32)
        b_cols = jnp.where(col == 0, b_hi, jnp.where(col == 1, b2 - b_hi, 0.0)).astype(BF16)

        def prep(c, _):
            r = pl.multiple_of(c * ck, ck)
            kb_ref[c, :, 0:dh] = k_ref[pl.ds(r, ck), :].astype(BF16)
            kb_ref[c, :, dh:] = b_cols
            vt_ref[c] = v_ref[pl.ds(r, ck), :].T.astype(BF16)
            return 0
        lax.fori_loop(0, n_chunks, prep, 0)

    ones = jnp.where(lax.broadcasted_iota(jnp.int32, (BIAS_ROWS, tq), 0) < 2, 1.0, 0.0)
    q_t = jnp.concatenate([(q_ref[...] * (dh ** -0.5 * LOG2E)).T, ones], axis=0).astype(BF16)
    tri = _upper_ones(ck)
    order = list(reversed(range(per_q)))

    zss, spbs, sp0s, keeps = [], [], [], []
    for d in order:
        u = _dot(kb_ref[i * per_q + d], q_t[:, d * ck:])
        keep = lax.broadcasted_iota(jnp.int32, u.shape, 0) < lax.broadcasted_iota(jnp.int32, u.shape, 1)
        sp = _softplus2(u)
        spm = jnp.where(keep, sp, 0.0)
        zss.append(u - sp)
        spbs.append(spm.astype(BF16))
        sp0s.append(spm[0:1, :])
        keeps.append(keep)
    withins = [_dot(tri, spb) for spb in spbs]
    carry = jnp.zeros((1, tq), F32)
    acc_ref[...] = jnp.zeros_like(acc_ref)
    for d, zs, keep, sp0, within in zip(order, zss, keeps, sp0s, withins):
        lo = d * ck
        w = jnp.where(keep, jnp.exp2(zs - (within + carry[:, lo:])), 0.0)
        seen = carry[:, lo:] + within[0:1, :] + sp0
        carry = seen if lo == 0 else jnp.concatenate([carry[:, :lo], seen], axis=1)
        acc_ref[:, lo:] += _dot(vt_ref[i * per_q + d], w.astype(BF16))
    carry_ref[...] = carry

    def below(it, _):
        base = (i - 1 - it) * per_q
        zss, spbs, sp0s = [], [], []
        for d in order:
            u = _dot(kb_ref[base + d], q_t)
            sp = _softplus2(u)
            zss.append(u - sp)
            spbs.append(sp.astype(BF16))
            sp0s.append(sp[0:1, :])
        withins = [_dot(tri, spb) for spb in spbs]
        carry = carry_ref[...]
        pv = None
        for d, zs, sp0, within in zip(order, zss, sp0s, withins):
            w = jnp.exp2(zs - (within + carry))
            carry = carry + within[0:1, :] + sp0
            part = _dot(vt_ref[base + d], w.astype(BF16))
            pv = part if pv is None else pv + part
        carry_ref[...] = carry
        acc_ref[...] += pv
        return 0

    lax.fori_loop(0, i, below, 0)
    o_ref[...] = acc_ref[...].T.astype(o_ref.dtype)


def sb_prompt(p, b_sb, layer, n_heads, *, tq):
    s = p.shape[0]
    dh = SB_HEAD_DIM
    n_chunks = s // SB_KEY_CHUNK
    return pl.pallas_call(
        functools.partial(_sb_prompt_kernel, layer, n_chunks),
        grid=(n_heads, s // tq),
        in_specs=[
            pl.BlockSpec(memory_space=pltpu.SMEM),
            pl.BlockSpec((tq, dh), lambda h, i: (i, h)),
            pl.BlockSpec((s, dh), lambda h, i: (0, n_heads + h)),
            pl.BlockSpec((s, dh), lambda h, i: (0, 2 * n_heads + h)),
        ],
        out_specs=pl.BlockSpec((tq, dh), lambda h, i: (i, h)),
        out_shape=jax.ShapeDtypeStruct((s, n_heads * dh), BF16),
        scratch_shapes=[
            pltpu.VMEM((n_chunks, SB_KEY_CHUNK, dh + BIAS_ROWS), BF16),
            pltpu.VMEM((n_chunks, dh, SB_KEY_CHUNK), BF16),
            pltpu.VMEM((dh, tq), F32),
            pltpu.VMEM((1, tq), F32),
        ],
        compiler_params=_params("arbitrary", "arbitrary"),
        name="sb_prompt",
    )(b_sb, p, p, p)


DECODE_LANES = 128
DECODE_ROWS_PER_T = 8
DECODE_ROWS_SHIFT = 3
MASKED_SCORE = -1e30


def _log2(n):
    assert n & (n - 1) == 0
    return n.bit_length() - 1


def _block_diag_queries(q_ref, n_t, n_heads, scale):
    d = q_ref.shape[-1]
    hd_shift = _log2(d // n_heads)
    r = lax.broadcasted_iota(jnp.int32, (DECODE_LANES, d), 0)
    c = lax.broadcasted_iota(jnp.int32, (DECODE_LANES, d), 1)
    t_of_r = r >> DECODE_ROWS_SHIFT
    h_of_r = r & (DECODE_ROWS_PER_T - 1)
    out = jnp.zeros((DECODE_LANES, d), F32)
    for t in range(n_t):
        out = jnp.where(t_of_r == t, q_ref[t:t + 1, :] * scale, out)
    return jnp.where(((c >> hd_shift) == h_of_r) & (t_of_r < n_t), out, 0.0)


def _take_block_diag(full, o_ref, n_t, n_heads):
    d = full.shape[-1]
    hd_shift = _log2(d // n_heads)
    r = lax.broadcasted_iota(jnp.int32, (DECODE_ROWS_PER_T, d), 0)
    c = lax.broadcasted_iota(jnp.int32, (DECODE_ROWS_PER_T, d), 1)
    own = (c >> hd_shift) == r
    for t in range(n_t):
        rows = full[t * DECODE_ROWS_PER_T:(t + 1) * DECODE_ROWS_PER_T, :]
        o_ref[t:t + 1, :] = jnp.sum(jnp.where(own, rows, 0.0), axis=0, keepdims=True).astype(o_ref.dtype)


NEW_KEY_ROWS = 16


def _sb_sample_kernel(n_pages, n_heads, pt_ref, bl_ref, q_ref, kn_ref, vn_ref, *refs):
    k_pages = refs[:n_pages]
    v_pages = refs[n_pages:2 * n_pages]
    o_ref, kf_ref, vf_ref, kx_ref, vx_ref = refs[2 * n_pages:]
    n_t, d = q_ref.shape
    dh = d // n_heads
    ck = SB_KEY_CHUNK
    page = k_pages[0].shape[0] // n_heads
    n_past = n_pages * page // ck
    new0 = n_past * ck

    @pl.when(pl.program_id(0) == 0)
    def _():
        kx_ref[...] = jnp.zeros_like(kx_ref)
        vx_ref[...] = jnp.zeros_like(vx_ref)
        kf_ref[new0:, :] = jnp.zeros((ck, d), BF16)
        vf_ref[new0:, :] = jnp.zeros((ck, d), BF16)

    def relayout(src_refs, dst_ref):
        for pg, src in enumerate(src_refs):
            for h in range(n_heads):
                dst_ref[pg * page:(pg + 1) * page, h * dh:(h + 1) * dh] = (
                    src[pl.ds(h, page, stride=n_heads), :].astype(BF16))

    kx_ref[0:n_t, :] = kn_ref[...]
    kf_ref[new0:new0 + NEW_KEY_ROWS, :] = kx_ref[...].astype(BF16)
    vx_ref[0:n_t, :] = vn_ref[...]
    vf_ref[new0:new0 + NEW_KEY_ROWS, :] = vx_ref[...].astype(BF16)
    relayout(k_pages, kf_ref)
    relayout(v_pages, vf_ref)
    qbd_t = _block_diag_queries(q_ref, n_t, n_heads, dh ** -0.5).T.astype(BF16)
    zt = _dot(kf_ref[...], qbd_t) + bl_ref[...]

    rows = lax.broadcasted_iota(jnp.int32, zt.shape, 0)
    t_of_lane = lax.broadcasted_iota(jnp.int32, zt.shape, 1) >> DECODE_ROWS_SHIFT
    keep = rows - new0 < t_of_lane
    sp = _softplus(zt)
    spm = jnp.where(keep, sp, 0.0)
    zs = zt - sp
    tri = _upper_ones(ck)
    within = [_dot(tri, spm[c * ck:(c + 1) * ck].astype(BF16)) for c in range(n_past + 1)]
    carry = jnp.zeros((1, DECODE_LANES), F32)
    pieces = [None] * (n_past + 1)
    for c in reversed(range(n_past + 1)):
        lo = c * ck
        w = jnp.exp(zs[lo:lo + ck] - (within[c] + carry))
        if c == n_past:
            w = jnp.where(keep[lo:lo + ck], w, 0.0)
        pieces[c] = w.T.astype(BF16)
        carry = carry + within[c][0:1, :] + spm[lo:lo + 1, :]
    full = _dot(jnp.concatenate(pieces, axis=1), vf_ref[...])
    _take_block_diag(full, o_ref, n_t, n_heads)


def sb_sample(p3, cache_k, cache_v, page_table, bias_lanes, layer, n_heads):
    db, n_t, _ = p3.shape
    n_pages = page_table.shape[1]
    page_rows, dh = cache_k.shape[2:]
    page = page_rows // n_heads
    d = n_heads * dh
    assert (n_pages * page) % SB_KEY_CHUNK == 0 and SB_KEY_CHUNK % page == 0
    assert n_t * DECODE_ROWS_PER_T <= DECODE_LANES and n_heads <= DECODE_ROWS_PER_T and n_t <= NEW_KEY_ROWS
    n_chunks = n_pages * page // SB_KEY_CHUNK

    def page_spec(pg):
        return pl.BlockSpec((None, None, page_rows, dh), lambda b, pt: (layer, pt[b, pg], 0, 0))

    grid_spec = pltpu.PrefetchScalarGridSpec(
        num_scalar_prefetch=1,
        grid=(db,),
        in_specs=[
            pl.BlockSpec((None, 1, DECODE_LANES), lambda b, pt: (layer, 0, 0)),
            pl.BlockSpec((None, n_t, d), lambda b, pt: (b, 0, 0)),
            pl.BlockSpec((None, n_t, d), lambda b, pt: (b, 0, 1)),
            pl.BlockSpec((None, n_t, d), lambda b, pt: (b, 0, 2)),
        ] + [page_spec(pg) for pg in range(n_pages)] * 2,
        out_specs=pl.BlockSpec((None, n_t, d), lambda b, pt: (b, 0, 0)),
        scratch_shapes=[
            pltpu.VMEM(((n_chunks + 1) * SB_KEY_CHUNK, d), BF16),
            pltpu.VMEM(((n_chunks + 1) * SB_KEY_CHUNK, d), BF16),
            pltpu.VMEM((NEW_KEY_ROWS, d), F32),
            pltpu.VMEM((NEW_KEY_ROWS, d), F32),
        ],
    )
    return pl.pallas_call(
        functools.partial(_sb_sample_kernel, n_pages, n_heads),
        grid_spec=grid_spec,
        out_shape=jax.ShapeDtypeStruct((db, n_t, d), F32),
        compiler_params=_params("arbitrary"),
        name="sb_sample",
    )(page_table, bias_lanes, p3, p3, p3, *([cache_k] * n_pages), *([cache_v] * n_pages))


def _cross_sample_kernel(q_ref, mk_ref, mv_ref, o_ref):
    bb, n_t, d = q_ref.shape
    n_mem, n_heads, hd = mk_ref.shape[1:]
    n_rows = n_mem * n_heads
    q_rows = n_t * DECODE_ROWS_PER_T
    r = lax.broadcasted_iota(jnp.int32, (q_rows, hd), 0)
    row_h = lax.broadcasted_iota(jnp.int32, (n_rows, DECODE_LANES), 0) & (n_heads - 1)
    lane_h = lax.broadcasted_iota(jnp.int32, (n_rows, DECODE_LANES), 1) & (DECODE_ROWS_PER_T - 1)
    own_head = row_h == lane_h
    for b in range(bb):
        qcat = jnp.zeros((q_rows, hd), F32)
        for t in range(n_t):
            for h in range(n_heads):
                qcat = jnp.where(r == t * DECODE_ROWS_PER_T + h, q_ref[b, t:t + 1, h * hd:(h + 1) * hd], qcat)
        qcat = jnp.concatenate([qcat * hd ** -0.5, jnp.zeros((DECODE_LANES - q_rows, hd), F32)], axis=0)
        mk2 = mk_ref[b].reshape(n_rows, hd).astype(BF16)
        mv2 = mv_ref[b].reshape(n_rows, hd).astype(BF16)
        st = jnp.where(own_head, _dot_nt(mk2, qcat.astype(BF16)), MASKED_SCORE)
        e = jnp.exp(st - jnp.max(st, axis=0, keepdims=True))
        p = e / jnp.sum(e, axis=0, keepdims=True)
        out = _dot(p.T.astype(BF16), mv2)
        for t in range(n_t):
            for h in range(n_heads):
                row = t * DECODE_ROWS_PER_T + h
                o_ref[b, t:t + 1, h * hd:(h + 1) * hd] = out[row:row + 1, :]


def cross_sample(q3, mem_k, mem_v, layer, *, bb):
    db, n_t, d = q3.shape
    n_mem, n_heads, hd = mem_k.shape[2:]
    assert n_t * DECODE_ROWS_PER_T <= DECODE_LANES and n_heads <= DECODE_ROWS_PER_T
    assert n_heads & (n_heads - 1) == 0 and db % bb == 0
    mem_spec = pl.BlockSpec((None, bb, n_mem, n_heads, hd), lambda b: (layer, b, 0, 0, 0))
    return pl.pallas_call(
        _cross_sample_kernel,
        grid=(db // bb,),
        in_specs=[pl.BlockSpec((bb, n_t, d), lambda b: (b, 0, 0)), mem_spec, mem_spec],
        out_specs=pl.BlockSpec((bb, n_t, d), lambda b: (b, 0, 0)),
        out_shape=jax.ShapeDtypeStruct((db, n_t, d), F32),
        compiler_params=_params("arbitrary"),
        name="cross_sample",
    )(q3, mem_k, mem_v)


def _ln_swish(c, g, b):
    mu = jnp.mean(c, axis=-1, keepdims=True)
    xc = c - mu
    var = jnp.mean(xc * xc, axis=-1, keepdims=True)
    y = xc * lax.rsqrt(var + EPS) * g + b
    return y * _sigmoid(y)


def _conv_prompt_kernel(width, halo, a_ref, gt_ref, ah_ref, gh_ref, w_ref, bd_ref, g_ref, b_ref,
                        o_ref, tail_ref, u_ref):
    i = pl.program_id(0)
    tb = a_ref.shape[0]
    u_halo = ah_ref[...] * _sigmoid(gh_ref[...])
    u_ref[0, 0:halo, :] = jnp.where(i == 0, 0.0, u_halo)
    u_ref[0, halo:halo + tb, :] = a_ref[...] * _sigmoid(gt_ref[...])
    n_rows = halo + tb - SUBLANES
    for s in range(1, SUBLANES):
        u_ref[s, 0:n_rows, :] = u_ref[0, s:s + n_rows, :]
    first = halo - (width - 1)
    rc = CONV_ROW_CHUNK

    for r0 in range(0, tb, rc):
        acc = jnp.broadcast_to(bd_ref[...], (rc, bd_ref.shape[-1]))
        for w in range(width):
            s, base = (first + w) % SUBLANES, (first + w) // SUBLANES * SUBLANES
            acc = acc + u_ref[s, r0 + base:r0 + base + rc, :] * w_ref[w:w + 1, :]
        o_ref[r0:r0 + rc, :] = _ln_swish(acc, g_ref[...], b_ref[...]).astype(o_ref.dtype)

    @pl.when(i == pl.num_programs(0) - 1)
    def _():
        tail_ref[...] = u_ref[0, tb:tb + halo, :]


def conv_prompt(p, w_dw, b_dw3, g3, b3, layer, *, tb):
    s = p.shape[0]
    width, c = w_dw.shape[1:]
    halo = 32
    assert width - 1 <= halo and tb % halo == 0
    hb = tb // halo
    vec = pl.BlockSpec((None, 1, c), lambda i: (layer, 0, 0))
    return pl.pallas_call(
        functools.partial(_conv_prompt_kernel, width, halo),
        grid=(s // tb,),
        in_specs=[
            pl.BlockSpec((tb, c), lambda i: (i, 3)),
            pl.BlockSpec((tb, c), lambda i: (i, 4)),
            pl.BlockSpec((halo, c), lambda i: (jnp.maximum(i * hb - 1, 0), 3)),
            pl.BlockSpec((halo, c), lambda i: (jnp.maximum(i * hb - 1, 0), 4)),
            pl.BlockSpec((None, width, c), lambda i: (layer, 0, 0)),
            vec, vec, vec,
        ],
        out_specs=[pl.BlockSpec((tb, c), lambda i: (i, 0)), pl.BlockSpec((halo, c), lambda i: (0, 0))],
        out_shape=[jax.ShapeDtypeStruct((s, c), BF16), jax.ShapeDtypeStruct((halo, c), F32)],
        scratch_shapes=[pltpu.VMEM((SUBLANES, tb + halo, c), F32)],
        compiler_params=_params("arbitrary"),
        name="conv_prompt",
    )(p, p, p, p, w_dw, b_dw3, g3, b3)


def _conv_sample_kernel(width, first, a_ref, gt_ref, hist_ref, w_ref, bd_ref, g_ref, b_ref, *refs):
    o_ref, hist_out_ref = refs[:2] if first else refs[1:3]
    bb, n_t, c = a_ref.shape
    n_hist = width - 1
    u_new = [a_ref[:, t, :] * _sigmoid(gt_ref[:, t, :]) for t in range(n_t)]
    u_full = [hist_ref[j] for j in range(n_hist)] + u_new
    for t in range(n_t):
        acc = jnp.broadcast_to(bd_ref[...], (bb, c))
        for w in range(width):
            acc = acc + u_full[t + w] * w_ref[w:w + 1, :]
        o_ref[t] = _ln_swish(acc, g_ref[...], b_ref[...])
    for j in range(n_hist):
        hist_out_ref[j] = u_full[j + n_t]


def conv_sample(p3, hist_t, w_dw, b_dw3, g3, b3, layer, hist_prev, *, bb):
    db, n_t, _ = p3.shape
    width, c = w_dw.shape[1:]
    n_hist = width - 1
    depth = hist_t.shape[0]
    first = hist_prev is None
    vec = pl.BlockSpec((None, 1, c), lambda i: (layer, 0, 0))
    in_specs = [
        pl.BlockSpec((bb, n_t, c), lambda i: (i, 0, 3)),
        pl.BlockSpec((bb, n_t, c), lambda i: (i, 0, 4)),
        pl.BlockSpec((None, n_hist, bb, c), lambda i: (layer, 0, i, 0)),
        pl.BlockSpec((None, width, c), lambda i: (layer, 0, 0)),
        vec, vec, vec,
    ]
    args = [p3, p3, hist_t, w_dw, b_dw3, g3, b3]
    aliases = {}
    if not first:
        in_specs.append(pl.BlockSpec(memory_space=pl.ANY))
        args.append(hist_prev)
        aliases = {7: 1}
    return pl.pallas_call(
        functools.partial(_conv_sample_kernel, width, first),
        grid=(db // bb,),
        in_specs=in_specs,
        out_specs=[pl.BlockSpec((n_t, bb, c), lambda i: (0, i, 0)),
                   pl.BlockSpec((None, n_hist, bb, c), lambda i: (layer, 0, i, 0))],
        out_shape=[jax.ShapeDtypeStruct((n_t, db, c), F32), jax.ShapeDtypeStruct((depth, n_hist, db, c), F32)],
        input_output_aliases=aliases,
        compiler_params=_params("arbitrary"),
        name="conv_sample",
    )(*args)


def kernel(x_prompt, x_sample, cache_sb_k, cache_sb_v, state_conv, cache_mem_k, cache_mem_v, page_table, mem_prompt, g_mix, w_in, b_sb, w_dw, b_dw, g_cln, b_cln, w_out, g_cross, g_mem, w_xq, w_xk, w_xv, w_xo, g_ffn, w_gate, w_up, w_down, g_final):
    batch, seq, d = x_prompt.shape
    db, n_t, _ = x_sample.shape
    depth = w_in.shape[0]
    n_sb_heads = b_sb.shape[1]
    sb_dim = n_sb_heads * SB_HEAD_DIM
    conv_dim = w_dw.shape[2]
    n_hist = w_dw.shape[1] - 1
    n_mem, n_x_heads, x_hd = cache_mem_k.shape[2:]
    n_pool, page = cache_sb_k.shape[1:3]
    assert batch == 1 and w_in.shape[2] == 3 * sb_dim + 2 * conv_dim and sb_dim == conv_dim
    ms = db * n_t

    col = lambda a: a.reshape(depth, 1, a.shape[-1])
    g_mix3, g_cross3, g_mem3, g_ffn3 = col(g_mix), col(g_cross), col(g_mem), col(g_ffn)
    b_dw3, g_cln3, b_cln3 = col(b_dw), col(g_cln), col(b_cln)
    cache_k = cache_sb_k.reshape(depth, n_pool, page * n_sb_heads, SB_HEAD_DIM)
    cache_v = cache_sb_v.reshape(depth, n_pool, page * n_sb_heads, SB_HEAD_DIM)
    lane_h = jnp.arange(DECODE_LANES) % DECODE_ROWS_PER_T
    bias_lanes = jnp.where(lane_h < n_sb_heads, b_sb[:, jnp.minimum(lane_h, n_sb_heads - 1)], 0.0)
    bias_lanes = bias_lanes.reshape(depth, 1, DECODE_LANES).astype(F32)

    tm_p = 1024
    tn_s = 1024
    xp = x_prompt.reshape(seq, d)
    xs = x_sample.reshape(ms, d)
    mem = mem_prompt.reshape(n_mem, d)
    sks, svs, cvp, mkp, mvp = [], [], [], [], []
    kv_prompt = hist_sample = None
    hist_t = jnp.swapaxes(state_conv, 1, 2)
    for l in range(depth):
        p, *kv_prompt = in_proj_prompt(xp, g_mix3, w_in, l, n_sb_heads, SB_HEAD_DIM, kv_prompt, tm=tm_p, tn=512)
        o_sb = sb_prompt(p, b_sb, l, n_sb_heads, tq=PROMPT_Q_BLOCK)
        c, tail = conv_prompt(p, w_dw, b_dw3, g_cln3, b_cln3, l, tb=256)
        xp = matmul_residual([o_sb, c], w_out, l, xp, tm=2 * tm_p, tn=512)
        cvp.append(tail[tail.shape[0] - n_hist:])

        ps = norm_matmul(xs, g_mix3, w_in, l, tm=ms, tn=tn_s)
        ps3 = ps.reshape(db, n_t, ps.shape[-1])
        o_sb = sb_sample(ps3, cache_k, cache_v, page_table, bias_lanes, l, n_sb_heads)
        c, hist_sample = conv_sample(ps3, hist_t, w_dw, b_dw3, g_cln3, b_cln3, l, hist_sample, bb=8)
        c = jnp.swapaxes(c, 0, 1).reshape(ms, conv_dim)
        xs = matmul_residual([o_sb.reshape(ms, sb_dim), c], w_out, l, xs, tm=ms, tn=tn_s)
        sks.append(ps[:, sb_dim:2 * sb_dim])
        svs.append(ps[:, 2 * sb_dim:3 * sb_dim])

        mk = norm_matmul(mem, g_mem3, w_xk, l, tm=n_mem, tn=tn_s)
        mv = norm_matmul(mem, g_mem3, w_xv, l, tm=n_mem, tn=tn_s)
        mkp.append(mk)
        mvp.append(mv)
        xp = cross_shared_residual(xp, g_cross3, w_xq, mk, mv, w_xo, l, n_x_heads, tm=tm_p)
        qs = norm_matmul(xs, g_cross3, w_xq, l, tm=ms, tn=tn_s)
        o_x = cross_sample(qs.reshape(db, n_t, d), cache_mem_k, cache_mem_v, l, bb=4)
        xs = matmul_residual([o_x.reshape(ms, d)], w_xo, l, xs, tm=ms, tn=tn_s)

        g_last = g_final.reshape(1, d) if l == depth - 1 else None
        xp = ffn_residual(xp, g_ffn3, w_gate, w_up, w_down, l, tm=tm_p, tf=256, g_final2=g_last)
        xs = ffn_residual(xs, g_ffn3, w_gate, w_up, w_down, l, tm=ms, tf=512, g_final2=g_last)

    y_prompt = xp.reshape(batch, seq, d)
    y_sample = xs.reshape(db, n_t, d)
    sb_shape = lambda rows: (depth,) + rows + (n_sb_heads, SB_HEAD_DIM)
    return (
        y_prompt,
        y_sample,
        kv_prompt[0].reshape(sb_shape((batch, seq))),
        kv_prompt[1].reshape(sb_shape((batch, seq))),
        jnp.stack(sks).reshape(sb_shape((db, n_t))),
        jnp.stack(svs).reshape(sb_shape((db, n_t))),
        jnp.stack(cvp).reshape(depth, batch, n_hist, conv_dim),
        jnp.swapaxes(hist_sample, 1, 2),
        jnp.stack(mkp).reshape(depth, batch, n_mem, n_x_heads, x_hd),
        jnp.stack(mvp).reshape(depth, batch, n_mem, n_x_heads, x_hd),
    )
```

```python
import functools

import jax
import jax.numpy as jnp
from jax import lax
from jax.experimental import pallas as pl
from jax.experimental.pallas import tpu as pltpu

F32 = jnp.float32
BF16 = jnp.bfloat16
EPS = 1e-6

V7X_VMEM_LIMIT_BYTES = 56 * 1024 * 1024
SUBLANES = 8
SB_HEAD_DIM = 128
SB_KEY_CHUNK = 256
PROMPT_Q_BLOCK = 1024
CONV_ROW_CHUNK = 32


def _params(*semantics):
    return pltpu.CompilerParams(dimension_semantics=semantics, vmem_limit_bytes=V7X_VMEM_LIMIT_BYTES)


def _rms(x, g):
    return x * lax.rsqrt(jnp.mean(x * x, axis=-1, keepdims=True) + EPS) * g


def _rms_bf16(x, g):
    return _rms(x, g).astype(BF16)


def _sigmoid(x):
    return 1.0 / (1.0 + jnp.exp(-x))


def _neg_abs(z):
    sign = jnp.uint32(0x80000000)
    return lax.bitcast_convert_type(lax.bitcast_convert_type(z, jnp.uint32) | sign, F32)


def _softplus(z):
    return jnp.maximum(z, 0.0) + jnp.log(1.0 + jnp.exp(_neg_abs(z)))


def _dot(a, b):
    return jnp.dot(a, b, preferred_element_type=F32)


def _dot_nt(a, b):
    return lax.dot_general(a, b, (((1,), (1,)), ((), ())), preferred_element_type=F32)


def _norm_matmul_kernel(x_ref, g_ref, w_ref, o_ref, h_ref):
    @pl.when(pl.program_id(1) == 0)
    def _():
        h_ref[...] = _rms_bf16(x_ref[...], g_ref[...])

    o_ref[...] = _dot(h_ref[...], w_ref[...].astype(BF16)).astype(o_ref.dtype)


def norm_matmul(x, g3, w3, layer, *, tm, tn):
    m, k = x.shape
    n = w3.shape[-1]
    return pl.pallas_call(
        _norm_matmul_kernel,
        grid=(m // tm, n // tn),
        in_specs=[
            pl.BlockSpec((tm, k), lambda i, j: (i, 0), pipeline_mode=pl.Buffered(1)),
            pl.BlockSpec((None, 1, k), lambda i, j: (layer, 0, 0)),
            pl.BlockSpec((None, k, tn), lambda i, j: (layer, 0, j)),
        ],
        out_specs=pl.BlockSpec((tm, tn), lambda i, j: (i, j)),
        out_shape=jax.ShapeDtypeStruct((m, n), F32),
        scratch_shapes=[pltpu.VMEM((tm, k), BF16)],
        compiler_params=_params("arbitrary", "arbitrary"),
        name="norm_matmul",
    )(x, g3, w3)


def _in_proj_kernel(n_heads, dh, x_ref, g_ref, w_ref, k_in_ref, v_in_ref, p_ref, k_ref, v_ref, h_ref):
    tm, tn = p_ref.shape
    j = pl.program_id(1)

    @pl.when(j == 0)
    def _():
        h_ref[...] = _rms_bf16(x_ref[...], g_ref[...])

    o = _dot(h_ref[...], w_ref[...].astype(BF16))
    p_ref[...] = o
    heads_per_step = tn // dh
    group = n_heads // heads_per_step
    for s in range(group):
        for dst_ref, step in ((k_ref, group + s), (v_ref, 2 * group + s)):
            @pl.when(j == step)
            def _(dst_ref=dst_ref):
                for hh in range(heads_per_step):
                    dst_ref[pl.ds(s * heads_per_step + hh, tm, stride=n_heads), :] = o[:, hh * dh:(hh + 1) * dh]


def in_proj_prompt(x, g3, w3, layer, n_heads, dh, kv, *, tm, tn):
    m, k = x.shape
    depth, _, n = w3.shape
    assert (n_heads * dh) % tn == 0 and tn % dh == 0
    kv_shape = jax.ShapeDtypeStruct((depth, m * n_heads, dh), F32)
    assert kv[0].shape == kv_shape.shape and kv[1].shape == kv_shape.shape
    kv_spec = pl.BlockSpec((None, tm * n_heads, dh), lambda i, j: (layer, i, 0))
    in_specs = [
        pl.BlockSpec((tm, k), lambda i, j: (i, 0), pipeline_mode=pl.Buffered(1)),
        pl.BlockSpec((None, 1, k), lambda i, j: (layer, 0, 0)),
        pl.BlockSpec((None, k, tn), lambda i, j: (layer, 0, j)),
        pl.BlockSpec(memory_space=pl.ANY),
        pl.BlockSpec(memory_space=pl.ANY),
    ]
    return pl.pallas_call(
        functools.partial(_in_proj_kernel, n_heads, dh),
        grid=(m // tm, n // tn),
        in_specs=in_specs,
        out_specs=[pl.BlockSpec((tm, tn), lambda i, j: (i, j)), kv_spec, kv_spec],
        out_shape=[jax.ShapeDtypeStruct((m, n), F32), kv_shape, kv_shape],
        scratch_shapes=[pltpu.VMEM((tm, k), BF16)],
        input_output_aliases={3: 1, 4: 2},
        compiler_params=_params("arbitrary", "arbitrary"),
        name="in_proj_prompt",
    )(x, g3, w3, *kv)


def _matmul_residual_kernel(n_pieces, *refs):
    a_refs = refs[:n_pieces]
    w_refs = refs[n_pieces:2 * n_pieces]
    r_ref, o_ref = refs[2 * n_pieces:]
    acc = r_ref[...]
    for a_ref, w_ref in zip(a_refs, w_refs):
        acc = acc + _dot(a_ref[...].astype(BF16), w_ref[...].astype(BF16))
    o_ref[...] = acc


def matmul_residual(pieces, w3, layer, res, *, tm, tn):
    m = res.shape[0]
    n = w3.shape[-1]
    kp = pieces[0].shape[-1]
    n_pieces = len(pieces)
    in_specs = [pl.BlockSpec((tm, kp), lambda i, j: (i, 0)) for _ in pieces]
    in_specs += [pl.BlockSpec((None, kp, tn), functools.partial(lambda p, i, j: (layer, p, j), p))
                 for p in range(n_pieces)]
    in_specs += [pl.BlockSpec((tm, tn), lambda i, j: (i, j))]
    return pl.pallas_call(
        functools.partial(_matmul_residual_kernel, n_pieces),
        grid=(m // tm, n // tn),
        in_specs=in_specs,
        out_specs=pl.BlockSpec((tm, tn), lambda i, j: (i, j)),
        out_shape=jax.ShapeDtypeStruct((m, n), F32),
        compiler_params=_params("arbitrary", "arbitrary"),
        name="matmul_residual",
    )(*pieces, *([w3] * n_pieces), res)


def _ffn_kernel(final_norm, x_ref, g_ref, wg_ref, wu_ref, wd_ref, *refs):
    gf_ref = refs[0] if final_norm else None
    o_ref, h_ref = refs[-2:]

    @pl.when(pl.program_id(1) == 0)
    def _():
        x = x_ref[...]
        h_ref[...] = _rms_bf16(x, g_ref[...])
        o_ref[...] = x

    h = h_ref[...]
    gate = _dot(h, wg_ref[...].astype(BF16))
    up = _dot(h, wu_ref[...].astype(BF16))
    act = (gate * _sigmoid(gate) * up).astype(BF16)
    o_ref[...] += _dot(act, wd_ref[...].astype(BF16))

    if final_norm:
        @pl.when(pl.program_id(1) == pl.num_programs(1) - 1)
        def _():
            o_ref[...] = _rms(o_ref[...], gf_ref[...])


def ffn_residual(x, g3, wg3, wu3, wd3, layer, *, tm, tf, g_final2=None):
    m, d = x.shape
    f = wg3.shape[-1]
    final_norm = g_final2 is not None
    in_specs = [
        pl.BlockSpec((tm, d), lambda i, j: (i, 0), pipeline_mode=pl.Buffered(1)),
        pl.BlockSpec((None, 1, d), lambda i, j: (layer, 0, 0)),
        pl.BlockSpec((None, d, tf), lambda i, j: (layer, 0, j)),
        pl.BlockSpec((None, d, tf), lambda i, j: (layer, 0, j)),
        pl.BlockSpec((None, tf, d), lambda i, j: (layer, j, 0)),
    ]
    args = [x, g3, wg3, wu3, wd3]
    if final_norm:
        in_specs.append(pl.BlockSpec((1, d), lambda i, j: (0, 0)))
        args.append(g_final2)
    return pl.pallas_call(
        functools.partial(_ffn_kernel, final_norm),
        grid=(m // tm, f // tf),
        in_specs=in_specs,
        out_specs=pl.BlockSpec((tm, d), lambda i, j: (i, 0)),
        out_shape=jax.ShapeDtypeStruct((m, d), F32),
        scratch_shapes=[pltpu.VMEM((tm, d), BF16)],
        compiler_params=_params("arbitrary", "arbitrary"),
        name="ffn_residual",
    )(*args)


def _cross_shared_kernel(scale, x_ref, g_ref, wq_ref, mk_ref, mv_ref, wo_ref, o_ref, h_ref):
    @pl.when(pl.program_id(1) == 0)
    def _():
        x = x_ref[...]
        h_ref[...] = _rms_bf16(x, g_ref[...])
        o_ref[...] = x

    q = _dot(h_ref[...], wq_ref[...].astype(BF16))
    s = _dot_nt(q.astype(BF16), mk_ref[...].astype(BF16)) * scale
    e = jnp.exp(s - jnp.max(s, axis=-1, keepdims=True))
    p = e / jnp.sum(e, axis=-1, keepdims=True)
    o = _dot(p.astype(BF16), mv_ref[...].astype(BF16))
    o_ref[...] += _dot(o.astype(BF16), wo_ref[...].astype(BF16))


def cross_shared_residual(x, g3, wq3, mk, mv, wo3, layer, n_heads, *, tm):
    m, d = x.shape
    hd = d // n_heads
    n_mem = mk.shape[0]
    return pl.pallas_call(
        functools.partial(_cross_shared_kernel, hd ** -0.5),
        grid=(m // tm, n_heads),
        in_specs=[
            pl.BlockSpec((tm, d), lambda i, j: (i, 0), pipeline_mode=pl.Buffered(1)),
            pl.BlockSpec((None, 1, d), lambda i, j: (layer, 0, 0)),
            pl.BlockSpec((None, d, hd), lambda i, j: (layer, 0, j)),
            pl.BlockSpec((n_mem, hd), lambda i, j: (0, j)),
            pl.BlockSpec((n_mem, hd), lambda i, j: (0, j)),
            pl.BlockSpec((None, hd, d), lambda i, j: (layer, j, 0)),
        ],
        out_specs=pl.BlockSpec((tm, d), lambda i, j: (i, 0)),
        out_shape=jax.ShapeDtypeStruct((m, d), F32),
        scratch_shapes=[pltpu.VMEM((tm, d), BF16)],
        compiler_params=_params("arbitrary", "arbitrary"),
        name="cross_shared_residual",
    )(x, g3, wq3, mk, mv, wo3)


def _upper_ones(n):
    r = lax.broadcasted_iota(jnp.int32, (n, n), 0)
    c = lax.broadcasted_iota(jnp.int32, (n, n), 1)
    return jnp.where(c > r, 1.0, 0.0).astype(BF16)


BIAS_ROWS = 16
LOG2E = 1.4426950408889634


def _softplus2(u):
    return jnp.maximum(u, 0.0) + jnp.log(1.0 + jnp.exp2(_neg_abs(u))) * LOG2E


def _sb_prompt_kernel(layer, n_chunks, b_ref, q_ref, k_ref, v_ref, o_ref, kb_ref, vt_ref, acc_ref, carry_ref):
    h = pl.program_id(0)
    i = pl.program_id(1)
    tq, dh = q_ref.shape
    ck = SB_KEY_CHUNK
    per_q = tq // ck

    @pl.when(i == 0)
    def _():
        b2 = jnp.full((ck, BIAS_ROWS), b_ref[layer, h] * LOG2E, F32)
        col = lax.broadcasted_iota(jnp.int32, (ck, BIAS_ROWS), 1)
        b_hi = b2.astype(BF16).astype(F32)
        b_cols = jnp.where(col == 0, b_hi, jnp.where(col == 1, b2 - b_hi, 0.0)).astype(BF16)

        def prep(c, _):
            r = pl.multiple_of(c * ck, ck)
            kb_ref[c, :, 0:dh] = k_ref[pl.ds(r, ck), :].astype(BF16)
            kb_ref[c, :, dh:] = b_cols
            vt_ref[c] = v_ref[pl.ds(r, ck), :].T.astype(BF16)
            return 0
        lax.fori_loop(0, n_chunks, prep, 0)

    ones = jnp.where(lax.broadcasted_iota(jnp.int32, (BIAS_ROWS, tq), 0) < 2, 1.0, 0.0)
    q_t = jnp.concatenate([(q_ref[...] * (dh ** -0.5 * LOG2E)).T, ones], axis=0).astype(BF16)
    tri = _upper_ones(ck)
    order = list(reversed(range(per_q)))

    zss, spbs, sp0s, keeps = [], [], [], []
    for d in order:
        u = _dot(kb_ref[i * per_q + d], q_t[:, d * ck:])
        keep = lax.broadcasted_iota(jnp.int32, u.shape, 0) < lax.broadcasted_iota(jnp.int32, u.shape, 1)
        sp = _softplus2(u)
        spm = jnp.where(keep, sp, 0.0)
        zss.append(u - sp)
        spbs.append(spm.astype(BF16))
        sp0s.append(spm[0:1, :])
        keeps.append(keep)
    withins = [_dot(tri, spb) for spb in spbs]
    carry = jnp.zeros((1, tq), F32)
    acc_ref[...] = jnp.zeros_like(acc_ref)
    for d, zs, keep, sp0, within in zip(order, zss, keeps, sp0s, withins):
        lo = d * ck
        w = jnp.where(keep, jnp.exp2(zs - (within + carry[:, lo:])), 0.0)
        seen = carry[:, lo:] + within[0:1, :] + sp0
        carry = seen if lo == 0 else jnp.concatenate([carry[:, :lo], seen], axis=1)
        acc_ref[:, lo:] += _dot(vt_ref[i * per_q + d], w.astype(BF16))
    carry_ref[...] = carry

    def below(it, _):
        base = (i - 1 - it) * per_q
        zss, spbs, sp0s = [], [], []
        for d in order:
            u = _dot(kb_ref[base + d], q_t)
            sp = _softplus2(u)
            zss.append(u - sp)
            spbs.append(sp.astype(BF16))
            sp0s.append(sp[0:1, :])
        withins = [_dot(tri, spb) for spb in spbs]
        carry = carry_ref[...]
        pv = None
        for d, zs, sp0, within in zip(order, zss, sp0s, withins):
            w = jnp.exp2(zs - (within + carry))
            carry = carry + within[0:1, :] + sp0
            part = _dot(vt_ref[base + d], w.astype(BF16))
            pv = part if pv is None else pv + part
        carry_ref[...] = carry
        acc_ref[...] += pv
        return 0

    lax.fori_loop(0, i, below, 0)
    o_ref[...] = acc_ref[...].T.astype(o_ref.dtype)


def sb_prompt(p, b_sb, layer, n_heads, *, tq):
    s = p.shape[0]
    dh = SB_HEAD_DIM
    n_chunks = s // SB_KEY_CHUNK
    return pl.pallas_call(
        functools.partial(_sb_prompt_kernel, layer, n_chunks),
        grid=(n_heads, s // tq),
        in_specs=[
            pl.BlockSpec(memory_space=pltpu.SMEM),
            pl.BlockSpec((tq, dh), lambda h, i: (i, h)),
            pl.BlockSpec((s, dh), lambda h, i: (0, n_heads + h)),
            pl.BlockSpec((s, dh), lambda h, i: (0, 2 * n_heads + h)),
        ],
        out_specs=pl.BlockSpec((tq, dh), lambda h, i: (i, h)),
        out_shape=jax.ShapeDtypeStruct((s, n_heads * dh), BF16),
        scratch_shapes=[
            pltpu.VMEM((n_chunks, SB_KEY_CHUNK, dh + BIAS_ROWS), BF16),
            pltpu.VMEM((n_chunks, dh, SB_KEY_CHUNK), BF16),
            pltpu.VMEM((dh, tq), F32),
            pltpu.VMEM((1, tq), F32),
        ],
        compiler_params=_params("arbitrary", "arbitrary"),
        name="sb_prompt",
    )(b_sb, p, p, p)


DECODE_LANES = 128
DECODE_ROWS_PER_T = 8
DECODE_ROWS_SHIFT = 3
MASKED_SCORE = -1e30


def _log2(n):
    assert n & (n - 1) == 0
    return n.bit_length() - 1


def _block_diag_queries(q_ref, n_t, n_heads, scale):
    d = q_ref.shape[-1]
    hd_shift = _log2(d // n_heads)
    r = lax.broadcasted_iota(jnp.int32, (DECODE_LANES, d), 0)
    c = lax.broadcasted_iota(jnp.int32, (DECODE_LANES, d), 1)
    t_of_r = r >> DECODE_ROWS_SHIFT
    h_of_r = r & (DECODE_ROWS_PER_T - 1)
    out = jnp.zeros((DECODE_LANES, d), F32)
    for t in range(n_t):
        out = jnp.where(t_of_r == t, q_ref[t:t + 1, :] * scale, out)
    return jnp.where(((c >> hd_shift) == h_of_r) & (t_of_r < n_t), out, 0.0)


def _take_block_diag(full, o_ref, n_t, n_heads):
    d = full.shape[-1]
    hd_shift = _log2(d // n_heads)
    r = lax.broadcasted_iota(jnp.int32, (DECODE_ROWS_PER_T, d), 0)
    c = lax.broadcasted_iota(jnp.int32, (DECODE_ROWS_PER_T, d), 1)
    own = (c >> hd_shift) == r
    for t in range(n_t):
        rows = full[t * DECODE_ROWS_PER_T:(t + 1) * DECODE_ROWS_PER_T, :]
        o_ref[t:t + 1, :] = jnp.sum(jnp.where(own, rows, 0.0), axis=0, keepdims=True).astype(o_ref.dtype)


NEW_KEY_ROWS = 16


def _sb_sample_kernel(n_pages, n_heads, pt_ref, bl_ref, q_ref, kn_ref, vn_ref, *refs):
    k_pages = refs[:n_pages]
    v_pages = refs[n_pages:2 * n_pages]
    o_ref, kf_ref, vf_ref, kx_ref, vx_ref = refs[2 * n_pages:]
    n_t, d = q_ref.shape
    dh = d // n_heads
    ck = SB_KEY_CHUNK
    page = k_pages[0].shape[0] // n_heads
    n_past = n_pages * page // ck
    new0 = n_past * ck

    @pl.when(pl.program_id(0) == 0)
    def _():
        kx_ref[...] = jnp.zeros_like(kx_ref)
        vx_ref[...] = jnp.zeros_like(vx_ref)
        kf_ref[new0:, :] = jnp.zeros((ck, d), BF16)
        vf_ref[new0:, :] = jnp.zeros((ck, d), BF16)

    def relayout(src_refs, dst_ref):
        for pg, src in enumerate(src_refs):
            for h in range(n_heads):
                dst_ref[pg * page:(pg + 1) * page, h * dh:(h + 1) * dh] = (
                    src[pl.ds(h, page, stride=n_heads), :].astype(BF16))

    kx_ref[0:n_t, :] = kn_ref[...]
    kf_ref[new0:new0 + NEW_KEY_ROWS, :] = kx_ref[...].astype(BF16)
    vx_ref[0:n_t, :] = vn_ref[...]
    vf_ref[new0:new0 + NEW_KEY_ROWS, :] = vx_ref[...].astype(BF16)
    relayout(k_pages, kf_ref)
    relayout(v_pages, vf_ref)
    qbd_t = _block_diag_queries(q_ref, n_t, n_heads, dh ** -0.5).T.astype(BF16)
    zt = _dot(kf_ref[...], qbd_t) + bl_ref[...]

    rows = lax.broadcasted_iota(jnp.int32, zt.shape, 0)
    t_of_lane = lax.broadcasted_iota(jnp.int32, zt.shape, 1) >> DECODE_ROWS_SHIFT
    keep = rows - new0 < t_of_lane
    sp = _softplus(zt)
    spm = jnp.where(keep, sp, 0.0)
    zs = zt - sp
    tri = _upper_ones(ck)
    within = [_dot(tri, spm[c * ck:(c + 1) * ck].astype(BF16)) for c in range(n_past + 1)]
    carry = jnp.zeros((1, DECODE_LANES), F32)
    pieces = [None] * (n_past + 1)
    for c in reversed(range(n_past + 1)):
        lo = c * ck
        w = jnp.exp(zs[lo:lo + ck] - (within[c] + carry))
        if c == n_past:
            w = jnp.where(keep[lo:lo + ck], w, 0.0)
        pieces[c] = w.T.astype(BF16)
        carry = carry + within[c][0:1, :] + spm[lo:lo + 1, :]
    full = _dot(jnp.concatenate(pieces, axis=1), vf_ref[...])
    _take_block_diag(full, o_ref, n_t, n_heads)


def sb_sample(p3, cache_k, cache_v, page_table, bias_lanes, layer, n_heads):
    db, n_t, _ = p3.shape
    n_pages = page_table.shape[1]
    page_rows, dh = cache_k.shape[2:]
    page = page_rows // n_heads
    d = n_heads * dh
    assert (n_pages * page) % SB_KEY_CHUNK == 0 and SB_KEY_CHUNK % page == 0
    assert n_t * DECODE_ROWS_PER_T <= DECODE_LANES and n_heads <= DECODE_ROWS_PER_T and n_t <= NEW_KEY_ROWS
    n_chunks = n_pages * page // SB_KEY_CHUNK

    def page_spec(pg):
        return pl.BlockSpec((None, None, page_rows, dh), lambda b, pt: (layer, pt[b, pg], 0, 0))

    grid_spec = pltpu.PrefetchScalarGridSpec(
        num_scalar_prefetch=1,
        grid=(db,),
        in_specs=[
            pl.BlockSpec((None, 1, DECODE_LANES), lambda b, pt: (layer, 0, 0)),
            pl.BlockSpec((None, n_t, d), lambda b, pt: (b, 0, 0)),
            pl.BlockSpec((None, n_t, d), lambda b, pt: (b, 0, 1)),
            pl.BlockSpec((None, n_t, d), lambda b, pt: (b, 0, 2)),
        ] + [page_spec(pg) for pg in range(n_pages)] * 2,
        out_specs=pl.BlockSpec((None, n_t, d), lambda b, pt: (b, 0, 0)),
        scratch_shapes=[
            pltpu.VMEM(((n_chunks + 1) * SB_KEY_CHUNK, d), BF16),
            pltpu.VMEM(((n_chunks + 1) * SB_KEY_CHUNK, d), BF16),
            pltpu.VMEM((NEW_KEY_ROWS, d), F32),
            pltpu.VMEM((NEW_KEY_ROWS, d), F32),
        ],
    )
    return pl.pallas_call(
        functools.partial(_sb_sample_kernel, n_pages, n_heads),
        grid_spec=grid_spec,
        out_shape=jax.ShapeDtypeStruct((db, n_t, d), F32),
        compiler_params=_params("arbitrary"),
        name="sb_sample",
    )(page_table, bias_lanes, p3, p3, p3, *([cache_k] * n_pages), *([cache_v] * n_pages))


def _cross_sample_kernel(q_ref, mk_ref, mv_ref, o_ref):
    bb, n_t, d = q_ref.shape
    n_mem, n_heads, hd = mk_ref.shape[1:]
    n_rows = n_mem * n_heads
    q_rows = n_t * DECODE_ROWS_PER_T
    r = lax.broadcasted_iota(jnp.int32, (q_rows, hd), 0)
    row_h = lax.broadcasted_iota(jnp.int32, (n_rows, DECODE_LANES), 0) & (n_heads - 1)
    lane_h = lax.broadcasted_iota(jnp.int32, (n_rows, DECODE_LANES), 1) & (DECODE_ROWS_PER_T - 1)
    own_head = row_h == lane_h
    for b in range(bb):
        qcat = jnp.zeros((q_rows, hd), F32)
        for t in range(n_t):
            for h in range(n_heads):
                qcat = jnp.where(r == t * DECODE_ROWS_PER_T + h, q_ref[b, t:t + 1, h * hd:(h + 1) * hd], qcat)
        qcat = jnp.concatenate([qcat * hd ** -0.5, jnp.zeros((DECODE_LANES - q_rows, hd), F32)], axis=0)
        mk2 = mk_ref[b].reshape(n_rows, hd).astype(BF16)
        mv2 = mv_ref[b].reshape(n_rows, hd).astype(BF16)
        st = jnp.where(own_head, _dot_nt(mk2, qcat.astype(BF16)), MASKED_SCORE)
        e = jnp.exp(st - jnp.max(st, axis=0, keepdims=True))
        p = e / jnp.sum(e, axis=0, keepdims=True)
        out = _dot(p.T.astype(BF16), mv2)
        for t in range(n_t):
            for h in range(n_heads):
                row = t * DECODE_ROWS_PER_T + h
                o_ref[b, t:t + 1, h * hd:(h + 1) * hd] = out[row:row + 1, :]


def cross_sample(q3, mem_k, mem_v, layer, *, bb):
    db, n_t, d = q3.shape
    n_mem, n_heads, hd = mem_k.shape[2:]
    assert n_t * DECODE_ROWS_PER_T <= DECODE_LANES and n_heads <= DECODE_ROWS_PER_T
    assert n_heads & (n_heads - 1) == 0 and db % bb == 0
    mem_spec = pl.BlockSpec((None, bb, n_mem, n_heads, hd), lambda b: (layer, b, 0, 0, 0))
    return pl.pallas_call(
        _cross_sample_kernel,
        grid=(db // bb,),
        in_specs=[pl.BlockSpec((bb, n_t, d), lambda b: (b, 0, 0)), mem_spec, mem_spec],
        out_specs=pl.BlockSpec((bb, n_t, d), lambda b: (b, 0, 0)),
        out_shape=jax.ShapeDtypeStruct((db, n_t, d), F32),
        compiler_params=_params("arbitrary"),
        name="cross_sample",
    )(q3, mem_k, mem_v)


def _ln_swish(c, g, b):
    mu = jnp.mean(c, axis=-1, keepdims=True)
    xc = c - mu
    var = jnp.mean(xc * xc, axis=-1, keepdims=True)
    y = xc * lax.rsqrt(var + EPS) * g + b
    return y * _sigmoid(y)


def _conv_prompt_kernel(width, halo, a_ref, gt_ref, ah_ref, gh_ref, w_ref, bd_ref, g_ref, b_ref,
                        o_ref, tail_ref, u_ref):
    i = pl.program_id(0)
    tb = a_ref.shape[0]
    u_halo = ah_ref[...] * _sigmoid(gh_ref[...])
    u_ref[0, 0:halo, :] = jnp.where(i == 0, 0.0, u_halo)
    u_ref[0, halo:halo + tb, :] = a_ref[...] * _sigmoid(gt_ref[...])
    n_rows = halo + tb - SUBLANES
    for s in range(1, SUBLANES):
        u_ref[s, 0:n_rows, :] = u_ref[0, s:s + n_rows, :]
    first = halo - (width - 1)
    rc = CONV_ROW_CHUNK

    for r0 in range(0, tb, rc):
        acc = jnp.broadcast_to(bd_ref[...], (rc, bd_ref.shape[-1]))
        for w in range(width):
            s, base = (first + w) % SUBLANES, (first + w) // SUBLANES * SUBLANES
            acc = acc + u_ref[s, r0 + base:r0 + base + rc, :] * w_ref[w:w + 1, :]
        o_ref[r0:r0 + rc, :] = _ln_swish(acc, g_ref[...], b_ref[...]).astype(o_ref.dtype)

    @pl.when(i == pl.num_programs(0) - 1)
    def _():
        tail_ref[...] = u_ref[0, tb:tb + halo, :]


def conv_prompt(p, w_dw, b_dw3, g3, b3, layer, *, tb):
    s = p.shape[0]
    width, c = w_dw.shape[1:]
    halo = 32
    assert width - 1 <= halo and tb % halo == 0
    hb = tb // halo
    vec = pl.BlockSpec((None, 1, c), lambda i: (layer, 0, 0))
    return pl.pallas_call(
        functools.partial(_conv_prompt_kernel, width, halo),
        grid=(s // tb,),
        in_specs=[
            pl.BlockSpec((tb, c), lambda i: (i, 3)),
            pl.BlockSpec((tb, c), lambda i: (i, 4)),
            pl.BlockSpec((halo, c), lambda i: (jnp.maximum(i * hb - 1, 0), 3)),
            pl.BlockSpec((halo, c), lambda i: (jnp.maximum(i * hb - 1, 0), 4)),
            pl.BlockSpec((None, width, c), lambda i: (layer, 0, 0)),
            vec, vec, vec,
        ],
        out_specs=[pl.BlockSpec((tb, c), lambda i: (i, 0)), pl.BlockSpec((halo, c), lambda i: (0, 0))],
        out_shape=[jax.ShapeDtypeStruct((s, c), BF16), jax.ShapeDtypeStruct((halo, c), F32)],
        scratch_shapes=[pltpu.VMEM((SUBLANES, tb + halo, c), F32)],
        compiler_params=_params("arbitrary"),
        name="conv_prompt",
    )(p, p, p, p, w_dw, b_dw3, g3, b3)


def _conv_sample_kernel(width, a_ref, gt_ref, hist_ref, w_ref, bd_ref, g_ref, b_ref, hist_in_ref, o_ref, hist_out_ref):
    bb, n_t, c = a_ref.shape
    n_hist = width - 1
    u_new = [a_ref[:, t, :] * _sigmoid(gt_ref[:, t, :]) for t in range(n_t)]
    u_full = [hist_ref[j] for j in range(n_hist)] + u_new
    for t in range(n_t):
        acc = jnp.broadcast_to(bd_ref[...], (bb, c))
        for w in range(width):
            acc = acc + u_full[t + w] * w_ref[w:w + 1, :]
        o_ref[t] = _ln_swish(acc, g_ref[...], b_ref[...])
    for j in range(n_hist):
        hist_out_ref[j] = u_full[j + n_t]


def conv_sample(p3, hist_t, w_dw, b_dw3, g3, b3, layer, hist_new, *, bb):
    db, n_t, _ = p3.shape
    width, c = w_dw.shape[1:]
    n_hist = width - 1
    depth = hist_t.shape[0]
    vec = pl.BlockSpec((None, 1, c), lambda i: (layer, 0, 0))
    in_specs = [
        pl.BlockSpec((bb, n_t, c), lambda i: (i, 0, 3)),
        pl.BlockSpec((bb, n_t, c), lambda i: (i, 0, 4)),
        pl.BlockSpec((None, n_hist, bb, c), lambda i: (layer, 0, i, 0)),
        pl.BlockSpec((None, width, c), lambda i: (layer, 0, 0)),
        vec, vec, vec,
    ]
    in_specs.append(pl.BlockSpec(memory_space=pl.ANY))
    return pl.pallas_call(
        functools.partial(_conv_sample_kernel, width),
        grid=(db // bb,),
        in_specs=in_specs,
        out_specs=[pl.BlockSpec((n_t, bb, c), lambda i: (0, i, 0)),
                   pl.BlockSpec((None, n_hist, bb, c), lambda i: (layer, 0, i, 0))],
        out_shape=[jax.ShapeDtypeStruct((n_t, db, c), F32), jax.ShapeDtypeStruct((depth, n_hist, db, c), F32)],
        input_output_aliases={7: 1},
        compiler_params=_params("arbitrary"),
        name="conv_sample",
    )(p3, p3, hist_t, w_dw, b_dw3, g3, b3, hist_new)


def kernel(x_prompt, x_sample, cache_sb_k, cache_sb_v, state_conv, cache_mem_k, cache_mem_v, page_table, mem_prompt, g_mix, w_in, b_sb, w_dw, b_dw, g_cln, b_cln, w_out, g_cross, g_mem, w_xq, w_xk, w_xv, w_xo, g_ffn, w_gate, w_up, w_down, g_final):
    batch, seq, d = x_prompt.shape
    db, n_t, _ = x_sample.shape
    depth = w_in.shape[0]
    n_sb_heads = b_sb.shape[1]
    sb_dim = n_sb_heads * SB_HEAD_DIM
    conv_dim = w_dw.shape[2]
    n_hist = w_dw.shape[1] - 1
    n_mem, n_x_heads, x_hd = cache_mem_k.shape[2:]
    n_pool, page = cache_sb_k.shape[1:3]
    assert batch == 1 and w_in.shape[2] == 3 * sb_dim + 2 * conv_dim and sb_dim == conv_dim
    ms = db * n_t

    col = lambda a: a.reshape(depth, 1, a.shape[-1])
    g_mix3, g_cross3, g_mem3, g_ffn3 = col(g_mix), col(g_cross), col(g_mem), col(g_ffn)
    b_dw3, g_cln3, b_cln3 = col(b_dw), col(g_cln), col(b_cln)
    cache_k = cache_sb_k.reshape(depth, n_pool, page * n_sb_heads, SB_HEAD_DIM)
    cache_v = cache_sb_v.reshape(depth, n_pool, page * n_sb_heads, SB_HEAD_DIM)
    lane_h = jnp.arange(DECODE_LANES) % DECODE_ROWS_PER_T
    bias_lanes = jnp.where(lane_h < n_sb_heads, b_sb[:, jnp.minimum(lane_h, n_sb_heads - 1)], 0.0)
    bias_lanes = bias_lanes.reshape(depth, 1, DECODE_LANES).astype(F32)

    tm_p = 1024
    tn_s = 1024
    xp = x_prompt.reshape(seq, d)
    xs = x_sample.reshape(ms, d)
    mem = mem_prompt.reshape(n_mem, d)
    sks, svs, cvp, mkp, mvp = [], [], [], [], []
    kv_prompt = [jnp.zeros((depth, seq * n_sb_heads, SB_HEAD_DIM), F32) for _ in range(2)]
    hist_sample = jnp.zeros((depth, n_hist, db, conv_dim), F32)
    hist_t = jnp.swapaxes(state_conv, 1, 2)
    for l in range(depth):
        p, *kv_prompt = in_proj_prompt(xp, g_mix3, w_in, l, n_sb_heads, SB_HEAD_DIM, kv_prompt, tm=tm_p, tn=512)
        o_sb = sb_prompt(p, b_sb, l, n_sb_heads, tq=PROMPT_Q_BLOCK)
        c, tail = conv_prompt(p, w_dw, b_dw3, g_cln3, b_cln3, l, tb=256)
        xp = matmul_residual([o_sb, c], w_out, l, xp, tm=2 * tm_p, tn=512)
        cvp.append(tail[tail.shape[0] - n_hist:])

        ps = norm_matmul(xs, g_mix3, w_in, l, tm=ms, tn=tn_s)
        ps3 = ps.reshape(db, n_t, ps.shape[-1])
        o_sb = sb_sample(ps3, cache_k, cache_v, page_table, bias_lanes, l, n_sb_heads)
        c, hist_sample = conv_sample(ps3, hist_t, w_dw, b_dw3, g_cln3, b_cln3, l, hist_sample, bb=8)
        c = jnp.swapaxes(c, 0, 1).reshape(ms, conv_dim)
        xs = matmul_residual([o_sb.reshape(ms, sb_dim), c], w_out, l, xs, tm=ms, tn=tn_s)
        sks.append(ps[:, sb_dim:2 * sb_dim])
        svs.append(ps[:, 2 * sb_dim:3 * sb_dim])

        mk = norm_matmul(mem, g_mem3, w_xk, l, tm=n_mem, tn=tn_s)
        mv = norm_matmul(mem, g_mem3, w_xv, l, tm=n_mem, tn=tn_s)
        mkp.append(mk)
        mvp.append(mv)
        xp = cross_shared_residual(xp, g_cross3, w_xq, mk, mv, w_xo, l, n_x_heads, tm=tm_p)
        qs = norm_matmul(xs, g_cross3, w_xq, l, tm=ms, tn=tn_s)
        o_x = cross_sample(qs.reshape(db, n_t, d), cache_mem_k, cache_mem_v, l, bb=4)
        xs = matmul_residual([o_x.reshape(ms, d)], w_xo, l, xs, tm=ms, tn=tn_s)

        g_last = g_final.reshape(1, d) if l == depth - 1 else None
        xp = ffn_residual(xp, g_ffn3, w_gate, w_up, w_down, l, tm=tm_p, tf=256, g_final2=g_last)
        xs = ffn_residual(xs, g_ffn3, w_gate, w_up, w_down, l, tm=ms, tf=512, g_final2=g_last)

    y_prompt = xp.reshape(batch, seq, d)
    y_sample = xs.reshape(db, n_t, d)
    sb_shape = lambda rows: (depth,) + rows + (n_sb_heads, SB_HEAD_DIM)
    return (
        y_prompt,
        y_sample,
        kv_prompt[0].reshape(sb_shape((batch, seq))),
        kv_prompt[1].reshape(sb_shape((batch, seq))),
        jnp.stack(sks).reshape(sb_shape((db, n_t))),
        jnp.stack(svs).reshape(sb_shape((db, n_t))),
        jnp.stack(cvp).reshape(depth, batch, n_hist, conv_dim),
        jnp.swapaxes(hist_sample, 1, 2),
        jnp.stack(mkp).reshape(depth, batch, n_mem, n_x_heads, x_hd),
        jnp.stack(mvp).reshape(depth, batch, n_mem, n_x_heads, x_hd),
    )
```

```python
import functools

import jax
import jax.numpy as jnp
from jax import lax
from jax.experimental import pallas as pl
from jax.experimental.pallas import tpu as pltpu

F32 = jnp.float32
BF16 = jnp.bfloat16
EPS = 1e-6

V7X_VMEM_LIMIT_BYTES = 56 * 1024 * 1024
SUBLANES = 8
SB_HEAD_DIM = 128
SB_KEY_CHUNK = 256
PROMPT_Q_BLOCK = 1024
CONV_ROW_CHUNK = 32


def _params(*semantics):
    return pltpu.CompilerParams(dimension_semantics=semantics, vmem_limit_bytes=V7X_VMEM_LIMIT_BYTES)


def _rms(x, g):
    return x * lax.rsqrt(jnp.mean(x * x, axis=-1, keepdims=True) + EPS) * g


def _rms_bf16(x, g):
    return _rms(x, g).astype(BF16)


def _sigmoid(x):
    return 1.0 / (1.0 + jnp.exp(-x))


def _neg_abs(z):
    sign = jnp.uint32(0x80000000)
    return lax.bitcast_convert_type(lax.bitcast_convert_type(z, jnp.uint32) | sign, F32)


def _softplus(z):
    return jnp.maximum(z, 0.0) + jnp.log(1.0 + jnp.exp(_neg_abs(z)))


def _dot(a, b):
    return jnp.dot(a, b, preferred_element_type=F32)


def _dot_nt(a, b):
    return lax.dot_general(a, b, (((1,), (1,)), ((), ())), preferred_element_type=F32)


def _norm_matmul_kernel(x_ref, g_ref, w_ref, o_ref, h_ref):
    @pl.when(pl.program_id(1) == 0)
    def _():
        h_ref[...] = _rms_bf16(x_ref[...], g_ref[...])

    o_ref[...] = _dot(h_ref[...], w_ref[...].astype(BF16)).astype(o_ref.dtype)


def norm_matmul(x, g3, w3, layer, *, tm, tn):
    m, k = x.shape
    n = w3.shape[-1]
    return pl.pallas_call(
        _norm_matmul_kernel,
        grid=(m // tm, n // tn),
        in_specs=[
            pl.BlockSpec((tm, k), lambda i, j: (i, 0), pipeline_mode=pl.Buffered(1)),
            pl.BlockSpec((None, 1, k), lambda i, j: (layer, 0, 0)),
            pl.BlockSpec((None, k, tn), lambda i, j: (layer, 0, j)),
        ],
        out_specs=pl.BlockSpec((tm, tn), lambda i, j: (i, j)),
        out_shape=jax.ShapeDtypeStruct((m, n), F32),
        scratch_shapes=[pltpu.VMEM((tm, k), BF16)],
        compiler_params=_params("arbitrary", "arbitrary"),
        name="norm_matmul",
    )(x, g3, w3)


def _in_proj_kernel(n_heads, dh, x_ref, g_ref, w_ref, k_in_ref, v_in_ref, p_ref, k_ref, v_ref, h_ref):
    tm, tn = p_ref.shape
    j = pl.program_id(1)

    @pl.when(j == 0)
    def _():
        h_ref[...] = _rms_bf16(x_ref[...], g_ref[...])

    o = _dot(h_ref[...], w_ref[...].astype(BF16))
    p_ref[...] = o
    heads_per_step = tn // dh
    group = n_heads // heads_per_step
    for s in range(group):
        for dst_ref, step in ((k_ref, group + s), (v_ref, 2 * group + s)):
            @pl.when(j == step)
            def _(dst_ref=dst_ref):
                for hh in range(heads_per_step):
                    dst_ref[pl.ds(s * heads_per_step + hh, tm, stride=n_heads), :] = o[:, hh * dh:(hh + 1) * dh]


def in_proj_prompt(x, g3, w3, layer, n_heads, dh, kv, *, tm, tn):
    m, k = x.shape
    depth, _, n = w3.shape
    assert (n_heads * dh) % tn == 0 and tn % dh == 0
    kv_shape = jax.ShapeDtypeStruct((depth, m * n_heads, dh), F32)
    assert kv[0].shape == kv_shape.shape and kv[1].shape == kv_shape.shape
    kv_spec = pl.BlockSpec((None, tm * n_heads, dh), lambda i, j: (layer, i, 0))
    in_specs = [
        pl.BlockSpec((tm, k), lambda i, j: (i, 0)),
        pl.BlockSpec((None, 1, k), lambda i, j: (layer, 0, 0)),
        pl.BlockSpec((None, k, tn), lambda i, j: (layer, 0, j)),
        pl.BlockSpec(memory_space=pl.ANY),
        pl.BlockSpec(memory_space=pl.ANY),
    ]
    return pl.pallas_call(
        functools.partial(_in_proj_kernel, n_heads, dh),
        grid=(m // tm, n // tn),
        in_specs=in_specs,
        out_specs=[pl.BlockSpec((tm, tn), lambda i, j: (i, j)), kv_spec, kv_spec],
        out_shape=[jax.ShapeDtypeStruct((m, n), F32), kv_shape, kv_shape],
        scratch_shapes=[pltpu.VMEM((tm, k), BF16)],
        input_output_aliases={3: 1, 4: 2},
        compiler_params=_params("arbitrary", "arbitrary"),
        name="in_proj_prompt",
    )(x, g3, w3, *kv)


def _matmul_residual_kernel(n_pieces, *refs):
    a_refs = refs[:n_pieces]
    w_refs = refs[n_pieces:2 * n_pieces]
    r_ref, o_ref = refs[2 * n_pieces:]
    acc = r_ref[...]
    for a_ref, w_ref in zip(a_refs, w_refs):
        acc = acc + _dot(a_ref[...].astype(BF16), w_ref[...].astype(BF16))
    o_ref[...] = acc


def matmul_residual(pieces, w3, layer, res, *, tm, tn):
    m = res.shape[0]
    n = w3.shape[-1]
    kp = pieces[0].shape[-1]
    n_pieces = len(pieces)
    in_specs = [pl.BlockSpec((tm, kp), lambda i, j: (i, 0)) for _ in pieces]
    in_specs += [pl.BlockSpec((None, kp, tn), functools.partial(lambda p, i, j: (layer, p, j), p))
                 for p in range(n_pieces)]
    in_specs += [pl.BlockSpec((tm, tn), lambda i, j: (i, j))]
    return pl.pallas_call(
        functools.partial(_matmul_residual_kernel, n_pieces),
        grid=(m // tm, n // tn),
        in_specs=in_specs,
        out_specs=pl.BlockSpec((tm, tn), lambda i, j: (i, j)),
        out_shape=jax.ShapeDtypeStruct((m, n), F32),
        compiler_params=_params("arbitrary", "arbitrary"),
        name="matmul_residual",
    )(*pieces, *([w3] * n_pieces), res)


def _ffn_kernel(final_norm, x_ref, g_ref, wg_ref, wu_ref, wd_ref, *refs):
    gf_ref = refs[0] if final_norm else None
    o_ref, h_ref = refs[-2:]

    @pl.when(pl.program_id(1) == 0)
    def _():
        x = x_ref[...]
        h_ref[...] = _rms_bf16(x, g_ref[...])
        o_ref[...] = x

    h = h_ref[...]
    gate = _dot(h, wg_ref[...].astype(BF16))
    up = _dot(h, wu_ref[...].astype(BF16))
    act = (gate * _sigmoid(gate) * up).astype(BF16)
    o_ref[...] += _dot(act, wd_ref[...].astype(BF16))

    if final_norm:
        @pl.when(pl.program_id(1) == pl.num_programs(1) - 1)
        def _():
            o_ref[...] = _rms(o_ref[...], gf_ref[...])


def ffn_residual(x, g3, wg3, wu3, wd3, layer, *, tm, tf, g_final2=None):
    m, d = x.shape
    f = wg3.shape[-1]
    final_norm = g_final2 is not None
    in_specs = [
        pl.BlockSpec((tm, d), lambda i, j: (i, 0)),
        pl.BlockSpec((None, 1, d), lambda i, j: (layer, 0, 0)),
        pl.BlockSpec((None, d, tf), lambda i, j: (layer, 0, j)),
        pl.BlockSpec((None, d, tf), lambda i, j: (layer, 0, j)),
        pl.BlockSpec((None, tf, d), lambda i, j: (layer, j, 0)),
    ]
    args = [x, g3, wg3, wu3, wd3]
    if final_norm:
        in_specs.append(pl.BlockSpec((1, d), lambda i, j: (0, 0)))
        args.append(g_final2)
    return pl.pallas_call(
        functools.partial(_ffn_kernel, final_norm),
        grid=(m // tm, f // tf),
        in_specs=in_specs,
        out_specs=pl.BlockSpec((tm, d), lambda i, j: (i, 0)),
        out_shape=jax.ShapeDtypeStruct((m, d), F32),
        scratch_shapes=[pltpu.VMEM((tm, d), BF16)],
        compiler_params=_params("arbitrary", "arbitrary"),
        name="ffn_residual",
    )(*args)


def _cross_shared_kernel(scale, x_ref, g_ref, wq_ref, mk_ref, mv_ref, wo_ref, o_ref, h_ref):
    @pl.when(pl.program_id(1) == 0)
    def _():
        x = x_ref[...]
        h_ref[...] = _rms_bf16(x, g_ref[...])
        o_ref[...] = x

    q = _dot(h_ref[...], wq_ref[...].astype(BF16))
    s = _dot_nt(q.astype(BF16), mk_ref[...].astype(BF16)) * scale
    e = jnp.exp(s - jnp.max(s, axis=-1, keepdims=True))
    p = e / jnp.sum(e, axis=-1, keepdims=True)
    o = _dot(p.astype(BF16), mv_ref[...].astype(BF16))
    o_ref[...] += _dot(o.astype(BF16), wo_ref[...].astype(BF16))


def cross_shared_residual(x, g3, wq3, mk, mv, wo3, layer, n_heads, *, tm):
    m, d = x.shape
    hd = d // n_heads
    n_mem = mk.shape[0]
    return pl.pallas_call(
        functools.partial(_cross_shared_kernel, hd ** -0.5),
        grid=(m // tm, n_heads),
        in_specs=[
            pl.BlockSpec((tm, d), lambda i, j: (i, 0), pipeline_mode=pl.Buffered(1)),
            pl.BlockSpec((None, 1, d), lambda i, j: (layer, 0, 0)),
            pl.BlockSpec((None, d, hd), lambda i, j: (layer, 0, j)),
            pl.BlockSpec((n_mem, hd), lambda i, j: (0, j)),
            pl.BlockSpec((n_mem, hd), lambda i, j: (0, j)),
            pl.BlockSpec((None, hd, d), lambda i, j: (layer, j, 0)),
        ],
        out_specs=pl.BlockSpec((tm, d), lambda i, j: (i, 0)),
        out_shape=jax.ShapeDtypeStruct((m, d), F32),
        scratch_shapes=[pltpu.VMEM((tm, d), BF16)],
        compiler_params=_params("arbitrary", "arbitrary"),
        name="cross_shared_residual",
    )(x, g3, wq3, mk, mv, wo3)


def _upper_ones(n):
    r = lax.broadcasted_iota(jnp.int32, (n, n), 0)
    c = lax.broadcasted_iota(jnp.int32, (n, n), 1)
    return jnp.where(c > r, 1.0, 0.0).astype(BF16)


BIAS_ROWS = 16
LOG2E = 1.4426950408889634


def _softplus2(u):
    return jnp.maximum(u, 0.0) + jnp.log(1.0 + jnp.exp2(_neg_abs(u))) * LOG2E


def _sb_prompt_kernel(layer, n_chunks, b_ref, q_ref, k_ref, v_ref, o_ref, kb_ref, vt_ref, acc_ref, carry_ref):
    h = pl.program_id(0)
    i = pl.program_id(1)
    tq, dh = q_ref.shape
    ck = SB_KEY_CHUNK
    per_q = tq // ck

    @pl.when(i == 0)
    def _():
        b2 = jnp.full((ck, BIAS_ROWS), b_ref[layer, h] * LOG2E, F32)
        col = lax.broadcasted_iota(jnp.int32, (ck, BIAS_ROWS), 1)
        b_hi = b2.astype(BF16).astype(F32)
        b_cols = jnp.where(col == 0, b_hi, jnp.where(col == 1, b2 - b_hi, 0.0)).astype(BF16)

        def prep(c, _):
            r = pl.multiple_of(c * ck, ck)
            kb_ref[c, :, 0:dh] = k_ref[pl.ds(r, ck), :].astype(BF16)
            kb_ref[c, :, dh:] = b_cols
            vt_ref[c] = v_ref[pl.ds(r, ck), :].T.astype(BF16)
            return 0
        lax.fori_loop(0, n_chunks, prep, 0)

    ones = jnp.where(lax.broadcasted_iota(jnp.int32, (BIAS_ROWS, tq), 0) < 2, 1.0, 0.0)
    q_t = jnp.concatenate([(q_ref[...] * (dh ** -0.5 * LOG2E)).T, ones], axis=0).astype(BF16)
    tri = _upper_ones(ck)
    order = list(reversed(range(per_q)))

    zss, spbs, sp0s, keeps = [], [], [], []
    for d in order:
        u = _dot(kb_ref[i * per_q + d], q_t[:, d * ck:])
        keep = lax.broadcasted_iota(jnp.int32, u.shape, 0) < lax.broadcasted_iota(jnp.int32, u.shape, 1)
        sp = _softplus2(u)
        spm = jnp.where(keep, sp, 0.0)
        zss.append(u - sp)
        spbs.append(spm.astype(BF16))
        sp0s.append(spm[0:1, :])
        keeps.append(keep)
    withins = [_dot(tri, spb) for spb in spbs]
    carry = jnp.zeros((1, tq), F32)
    acc_ref[...] = jnp.zeros_like(acc_ref)
    for d, zs, keep, sp0, within in zip(order, zss, keeps, sp0s, withins):
        lo = d * ck
        w = jnp.where(keep, jnp.exp2(zs - (within + carry[:, lo:])), 0.0)
        seen = carry[:, lo:] + within[0:1, :] + sp0
        carry = seen if lo == 0 else jnp.concatenate([carry[:, :lo], seen], axis=1)
        acc_ref[:, lo:] += _dot(vt_ref[i * per_q + d], w.astype(BF16))
    carry_ref[...] = carry

    def below(it, _):
        base = (i - 1 - it) * per_q
        zss, spbs, sp0s = [], [], []
        for d in order:
            u = _dot(kb_ref[base + d], q_t)
            sp = _softplus2(u)
            zss.append(u - sp)
            spbs.append(sp.astype(BF16))
            sp0s.append(sp[0:1, :])
        withins = [_dot(tri, spb) for spb in spbs]
        carry = carry_ref[...]
        pv = None
        for d, zs, sp0, within in zip(order, zss, sp0s, withins):
            w = jnp.exp2(zs - (within + carry))
            carry = carry + within[0:1, :] + sp0
            part = _dot(vt_ref[base + d], w.astype(BF16))
            pv = part if pv is None else pv + part
        carry_ref[...] = carry
        acc_ref[...] += pv
        return 0

    lax.fori_loop(0, i, below, 0)
    o_ref[...] = acc_ref[...].T.astype(o_ref.dtype)


def sb_prompt(p, b_sb, layer, n_heads, *, tq):
    s = p.shape[0]
    dh = SB_HEAD_DIM
    n_chunks = s // SB_KEY_CHUNK
    return pl.pallas_call(
        functools.partial(_sb_prompt_kernel, layer, n_chunks),
        grid=(n_heads, s // tq),
        in_specs=[
            pl.BlockSpec(memory_space=pltpu.SMEM),
            pl.BlockSpec((tq, dh), lambda h, i: (i, h)),
            pl.BlockSpec((s, dh), lambda h, i: (0, n_heads + h)),
            pl.BlockSpec((s, dh), lambda h, i: (0, 2 * n_heads + h)),
        ],
        out_specs=pl.BlockSpec((tq, dh), lambda h, i: (i, h)),
        out_shape=jax.ShapeDtypeStruct((s, n_heads * dh), BF16),
        scratch_shapes=[
            pltpu.VMEM((n_chunks, SB_KEY_CHUNK, dh + BIAS_ROWS), BF16),
            pltpu.VMEM((n_chunks, dh, SB_KEY_CHUNK), BF16),
            pltpu.VMEM((dh, tq), F32),
            pltpu.VMEM((1, tq), F32),
        ],
        compiler_params=_params("arbitrary", "arbitrary"),
        name="sb_prompt",
    )(b_sb, p, p, p)


DECODE_LANES = 128
DECODE_ROWS_PER_T = 8
DECODE_ROWS_SHIFT = 3
MASKED_SCORE = -1e30


def _log2(n):
    assert n & (n - 1) == 0
    return n.bit_length() - 1


def _block_diag_queries(q_ref, n_t, n_heads, scale):
    d = q_ref.shape[-1]
    hd_shift = _log2(d // n_heads)
    r = lax.broadcasted_iota(jnp.int32, (DECODE_LANES, d), 0)
    c = lax.broadcasted_iota(jnp.int32, (DECODE_LANES, d), 1)
    t_of_r = r >> DECODE_ROWS_SHIFT
    h_of_r = r & (DECODE_ROWS_PER_T - 1)
    out = jnp.zeros((DECODE_LANES, d), F32)
    for t in range(n_t):
        out = jnp.where(t_of_r == t, q_ref[t:t + 1, :] * scale, out)
    return jnp.where(((c >> hd_shift) == h_of_r) & (t_of_r < n_t), out, 0.0)


def _take_block_diag(full, o_ref, n_t, n_heads):
    d = full.shape[-1]
    hd_shift = _log2(d // n_heads)
    r = lax.broadcasted_iota(jnp.int32, (DECODE_ROWS_PER_T, d), 0)
    c = lax.broadcasted_iota(jnp.int32, (DECODE_ROWS_PER_T, d), 1)
    own = (c >> hd_shift) == r
    for t in range(n_t):
        rows = full[t * DECODE_ROWS_PER_T:(t + 1) * DECODE_ROWS_PER_T, :]
        o_ref[t:t + 1, :] = jnp.sum(jnp.where(own, rows, 0.0), axis=0, keepdims=True).astype(o_ref.dtype)


NEW_KEY_ROWS = 16


def _sb_sample_kernel(n_pages, n_heads, pt_ref, bl_ref, q_ref, kn_ref, vn_ref, *refs):
    k_pages = refs[:n_pages]
    v_pages = refs[n_pages:2 * n_pages]
    o_ref, kf_ref, vf_ref, kx_ref, vx_ref = refs[2 * n_pages:]
    n_t, d = q_ref.shape
    dh = d // n_heads
    ck = SB_KEY_CHUNK
    page = k_pages[0].shape[0] // n_heads
    n_past = n_pages * page // ck
    new0 = n_past * ck

    @pl.when(pl.program_id(0) == 0)
    def _():
        kx_ref[...] = jnp.zeros_like(kx_ref)
        vx_ref[...] = jnp.zeros_like(vx_ref)
        kf_ref[new0:, :] = jnp.zeros((ck, d), BF16)
        vf_ref[new0:, :] = jnp.zeros((ck, d), BF16)

    def relayout(src_refs, dst_ref):
        for pg, src in enumerate(src_refs):
            for h in range(n_heads):
                dst_ref[pg * page:(pg + 1) * page, h * dh:(h + 1) * dh] = (
                    src[pl.ds(h, page, stride=n_heads), :].astype(BF16))

    kx_ref[0:n_t, :] = kn_ref[...]
    kf_ref[new0:new0 + NEW_KEY_ROWS, :] = kx_ref[...].astype(BF16)
    vx_ref[0:n_t, :] = vn_ref[...]
    vf_ref[new0:new0 + NEW_KEY_ROWS, :] = vx_ref[...].astype(BF16)
    relayout(k_pages, kf_ref)
    relayout(v_pages, vf_ref)
    qbd_t = _block_diag_queries(q_ref, n_t, n_heads, dh ** -0.5).T.astype(BF16)
    zt = _dot(kf_ref[...], qbd_t) + bl_ref[...]

    rows = lax.broadcasted_iota(jnp.int32, zt.shape, 0)
    t_of_lane = lax.broadcasted_iota(jnp.int32, zt.shape, 1) >> DECODE_ROWS_SHIFT
    keep = rows - new0 < t_of_lane
    sp = _softplus(zt)
    spm = jnp.where(keep, sp, 0.0)
    zs = zt - sp
    tri = _upper_ones(ck)
    within = [_dot(tri, spm[c * ck:(c + 1) * ck].astype(BF16)) for c in range(n_past + 1)]
    carry = jnp.zeros((1, DECODE_LANES), F32)
    pieces = [None] * (n_past + 1)
    for c in reversed(range(n_past + 1)):
        lo = c * ck
        w = jnp.exp(zs[lo:lo + ck] - (within[c] + carry))
        if c == n_past:
            w = jnp.where(keep[lo:lo + ck], w, 0.0)
        pieces[c] = w.T.astype(BF16)
        carry = carry + within[c][0:1, :] + spm[lo:lo + 1, :]
    full = _dot(jnp.concatenate(pieces, axis=1), vf_ref[...])
    _take_block_diag(full, o_ref, n_t, n_heads)


def sb_sample(p3, cache_k, cache_v, page_table, bias_lanes, layer, n_heads):
    db, n_t, _ = p3.shape
    n_pages = page_table.shape[1]
    page_rows, dh = cache_k.shape[2:]
    page = page_rows // n_heads
    d = n_heads * dh
    assert (n_pages * page) % SB_KEY_CHUNK == 0 and SB_KEY_CHUNK % page == 0
    assert n_t * DECODE_ROWS_PER_T <= DECODE_LANES and n_heads <= DECODE_ROWS_PER_T and n_t <= NEW_KEY_ROWS
    n_chunks = n_pages * page // SB_KEY_CHUNK

    def page_spec(pg):
        return pl.BlockSpec((None, None, page_rows, dh), lambda b, pt: (layer, pt[b, pg], 0, 0))

    grid_spec = pltpu.PrefetchScalarGridSpec(
        num_scalar_prefetch=1,
        grid=(db,),
        in_specs=[
            pl.BlockSpec((None, 1, DECODE_LANES), lambda b, pt: (layer, 0, 0)),
            pl.BlockSpec((None, n_t, d), lambda b, pt: (b, 0, 0)),
            pl.BlockSpec((None, n_t, d), lambda b, pt: (b, 0, 1)),
            pl.BlockSpec((None, n_t, d), lambda b, pt: (b, 0, 2)),
        ] + [page_spec(pg) for pg in range(n_pages)] * 2,
        out_specs=pl.BlockSpec((None, n_t, d), lambda b, pt: (b, 0, 0)),
        scratch_shapes=[
            pltpu.VMEM(((n_chunks + 1) * SB_KEY_CHUNK, d), BF16),
            pltpu.VMEM(((n_chunks + 1) * SB_KEY_CHUNK, d), BF16),
            pltpu.VMEM((NEW_KEY_ROWS, d), F32),
            pltpu.VMEM((NEW_KEY_ROWS, d), F32),
        ],
    )
    return pl.pallas_call(
        functools.partial(_sb_sample_kernel, n_pages, n_heads),
        grid_spec=grid_spec,
        out_shape=jax.ShapeDtypeStruct((db, n_t, d), F32),
        compiler_params=_params("arbitrary"),
        name="sb_sample",
    )(page_table, bias_lanes, p3, p3, p3, *([cache_k] * n_pages), *([cache_v] * n_pages))


def _cross_sample_kernel(q_ref, mk_ref, mv_ref, o_ref):
    bb, n_t, d = q_ref.shape
    n_mem, n_heads, hd = mk_ref.shape[1:]
    n_rows = n_mem * n_heads
    q_rows = n_t * DECODE_ROWS_PER_T
    r = lax.broadcasted_iota(jnp.int32, (q_rows, hd), 0)
    row_h = lax.broadcasted_iota(jnp.int32, (n_rows, DECODE_LANES), 0) & (n_heads - 1)
    lane_h = lax.broadcasted_iota(jnp.int32, (n_rows, DECODE_LANES), 1) & (DECODE_ROWS_PER_T - 1)
    own_head = row_h == lane_h
    for b in range(bb):
        qcat = jnp.zeros((q_rows, hd), F32)
        for t in range(n_t):
            for h in range(n_heads):
                qcat = jnp.where(r == t * DECODE_ROWS_PER_T + h, q_ref[b, t:t + 1, h * hd:(h + 1) * hd], qcat)
        qcat = jnp.concatenate([qcat * hd ** -0.5, jnp.zeros((DECODE_LANES - q_rows, hd), F32)], axis=0)
        mk2 = mk_ref[b].reshape(n_rows, hd).astype(BF16)
        mv2 = mv_ref[b].reshape(n_rows, hd).astype(BF16)
        st = jnp.where(own_head, _dot_nt(mk2, qcat.astype(BF16)), MASKED_SCORE)
        e = jnp.exp(st - jnp.max(st, axis=0, keepdims=True))
        p = e / jnp.sum(e, axis=0, keepdims=True)
        out = _dot(p.T.astype(BF16), mv2)
        for t in range(n_t):
            for h in range(n_heads):
                row = t * DECODE_ROWS_PER_T + h
                o_ref[b, t:t + 1, h * hd:(h + 1) * hd] = out[row:row + 1, :]


def cross_sample(q3, mem_k, mem_v, layer, *, bb):
    db, n_t, d = q3.shape
    n_mem, n_heads, hd = mem_k.shape[2:]
    assert n_t * DECODE_ROWS_PER_T <= DECODE_LANES and n_heads <= DECODE_ROWS_PER_T
    assert n_heads & (n_heads - 1) == 0 and db % bb == 0
    mem_spec = pl.BlockSpec((None, bb, n_mem, n_heads, hd), lambda b: (layer, b, 0, 0, 0))
    return pl.pallas_call(
        _cross_sample_kernel,
        grid=(db // bb,),
        in_specs=[pl.BlockSpec((bb, n_t, d), lambda b: (b, 0, 0)), mem_spec, mem_spec],
        out_specs=pl.BlockSpec((bb, n_t, d), lambda b: (b, 0, 0)),
        out_shape=jax.ShapeDtypeStruct((db, n_t, d), F32),
        compiler_params=_params("arbitrary"),
        name="cross_sample",
    )(q3, mem_k, mem_v)


def _ln_swish(c, g, b):
    mu = jnp.mean(c, axis=-1, keepdims=True)
    xc = c - mu
    var = jnp.mean(xc * xc, axis=-1, keepdims=True)
    y = xc * lax.rsqrt(var + EPS) * g + b
    return y * _sigmoid(y)


def _conv_prompt_kernel(width, halo, a_ref, gt_ref, ah_ref, gh_ref, w_ref, bd_ref, g_ref, b_ref,
                        o_ref, tail_ref, u_ref):
    i = pl.program_id(0)
    tb = a_ref.shape[0]
    u_halo = ah_ref[...] * _sigmoid(gh_ref[...])
    u_ref[0, 0:halo, :] = jnp.where(i == 0, 0.0, u_halo)
    u_ref[0, halo:halo + tb, :] = a_ref[...] * _sigmoid(gt_ref[...])
    n_rows = halo + tb - SUBLANES
    for s in range(1, SUBLANES):
        u_ref[s, 0:n_rows, :] = u_ref[0, s:s + n_rows, :]
    first = halo - (width - 1)
    rc = CONV_ROW_CHUNK

    for r0 in range(0, tb, rc):
        acc = jnp.broadcast_to(bd_ref[...], (rc, bd_ref.shape[-1]))
        for w in range(width):
            s, base = (first + w) % SUBLANES, (first + w) // SUBLANES * SUBLANES
            acc = acc + u_ref[s, r0 + base:r0 + base + rc, :] * w_ref[w:w + 1, :]
        o_ref[r0:r0 + rc, :] = _ln_swish(acc, g_ref[...], b_ref[...]).astype(o_ref.dtype)

    @pl.when(i == pl.num_programs(0) - 1)
    def _():
        tail_ref[...] = u_ref[0, tb:tb + halo, :]


def conv_prompt(p, w_dw, b_dw3, g3, b3, layer, *, tb):
    s = p.shape[0]
    width, c = w_dw.shape[1:]
    halo = 32
    assert width - 1 <= halo and tb % halo == 0
    hb = tb // halo
    vec = pl.BlockSpec((None, 1, c), lambda i: (layer, 0, 0))
    return pl.pallas_call(
        functools.partial(_conv_prompt_kernel, width, halo),
        grid=(s // tb,),
        in_specs=[
            pl.BlockSpec((tb, c), lambda i: (i, 3)),
            pl.BlockSpec((tb, c), lambda i: (i, 4)),
            pl.BlockSpec((halo, c), lambda i: (jnp.maximum(i * hb - 1, 0), 3)),
            pl.BlockSpec((halo, c), lambda i: (jnp.maximum(i * hb - 1, 0), 4)),
            pl.BlockSpec((None, width, c), lambda i: (layer, 0, 0)),
            vec, vec, vec,
        ],
        out_specs=[pl.BlockSpec((tb, c), lambda i: (i, 0)), pl.BlockSpec((halo, c), lambda i: (0, 0))],
        out_shape=[jax.ShapeDtypeStruct((s, c), BF16), jax.ShapeDtypeStruct((halo, c), F32)],
        scratch_shapes=[pltpu.VMEM((SUBLANES, tb + halo, c), F32)],
        compiler_params=_params("arbitrary"),
        name="conv_prompt",
    )(p, p, p, p, w_dw, b_dw3, g3, b3)


def _conv_sample_kernel(width, a_ref, gt_ref, hist_ref, w_ref, bd_ref, g_ref, b_ref, hist_in_ref, o_ref, hist_out_ref):
    bb, n_t, c = a_ref.shape
    n_hist = width - 1
    u_new = [a_ref[:, t, :] * _sigmoid(gt_ref[:, t, :]) for t in range(n_t)]
    u_full = [hist_ref[j] for j in range(n_hist)] + u_new
    for t in range(n_t):
        acc = jnp.broadcast_to(bd_ref[...], (bb, c))
        for w in range(width):
            acc = acc + u_full[t + w] * w_ref[w:w + 1, :]
        o_ref[t] = _ln_swish(acc, g_ref[...], b_ref[...])
    for j in range(n_hist):
        hist_out_ref[j] = u_full[j + n_t]


def conv_sample(p3, hist_t, w_dw, b_dw3, g3, b3, layer, hist_new, *, bb):
    db, n_t, _ = p3.shape
    width, c = w_dw.shape[1:]
    n_hist = width - 1
    depth = hist_t.shape[0]
    vec = pl.BlockSpec((None, 1, c), lambda i: (layer, 0, 0))
    in_specs = [
        pl.BlockSpec((bb, n_t, c), lambda i: (i, 0, 3)),
        pl.BlockSpec((bb, n_t, c), lambda i: (i, 0, 4)),
        pl.BlockSpec((None, n_hist, bb, c), lambda i: (layer, 0, i, 0)),
        pl.BlockSpec((None, width, c), lambda i: (layer, 0, 0)),
        vec, vec, vec,
    ]
    in_specs.append(pl.BlockSpec(memory_space=pl.ANY))
    return pl.pallas_call(
        functools.partial(_conv_sample_kernel, width),
        grid=(db // bb,),
        in_specs=in_specs,
        out_specs=[pl.BlockSpec((n_t, bb, c), lambda i: (0, i, 0)),
                   pl.BlockSpec((None, n_hist, bb, c), lambda i: (layer, 0, i, 0))],
        out_shape=[jax.ShapeDtypeStruct((n_t, db, c), F32), jax.ShapeDtypeStruct((depth, n_hist, db, c), F32)],
        input_output_aliases={7: 1},
        compiler_params=_params("arbitrary"),
        name="conv_sample",
    )(p3, p3, hist_t, w_dw, b_dw3, g3, b3, hist_new)


def kernel(x_prompt, x_sample, cache_sb_k, cache_sb_v, state_conv, cache_mem_k, cache_mem_v, page_table, mem_prompt, g_mix, w_in, b_sb, w_dw, b_dw, g_cln, b_cln, w_out, g_cross, g_mem, w_xq, w_xk, w_xv, w_xo, g_ffn, w_gate, w_up, w_down, g_final):
    batch, seq, d = x_prompt.shape
    db, n_t, _ = x_sample.shape
    depth = w_in.shape[0]
    n_sb_heads = b_sb.shape[1]
    sb_dim = n_sb_heads * SB_HEAD_DIM
    conv_dim = w_dw.shape[2]
    n_hist = w_dw.shape[1] - 1
    n_mem, n_x_heads, x_hd = cache_mem_k.shape[2:]
    n_pool, page = cache_sb_k.shape[1:3]
    assert batch == 1 and w_in.shape[2] == 3 * sb_dim + 2 * conv_dim and sb_dim == conv_dim
    ms = db * n_t

    col = lambda a: a.reshape(depth, 1, a.shape[-1])
    g_mix3, g_cross3, g_mem3, g_ffn3 = col(g_mix), col(g_cross), col(g_mem), col(g_ffn)
    b_dw3, g_cln3, b_cln3 = col(b_dw), col(g_cln), col(b_cln)
    cache_k = cache_sb_k.reshape(depth, n_pool, page * n_sb_heads, SB_HEAD_DIM)
    cache_v = cache_sb_v.reshape(depth, n_pool, page * n_sb_heads, SB_HEAD_DIM)
    lane_h = jnp.arange(DECODE_LANES) % DECODE_ROWS_PER_T
    bias_lanes = jnp.where(lane_h < n_sb_heads, b_sb[:, jnp.minimum(lane_h, n_sb_heads - 1)], 0.0)
    bias_lanes = bias_lanes.reshape(depth, 1, DECODE_LANES).astype(F32)

    tm_p = 1024
    tn_s = 1024
    xp = x_prompt.reshape(seq, d)
    xs = x_sample.reshape(ms, d)
    mem = mem_prompt.reshape(n_mem, d)
    sks, svs, cvp, mkp, mvp = [], [], [], [], []
    kv_prompt = [jnp.zeros((depth, seq * n_sb_heads, SB_HEAD_DIM), F32) for _ in range(2)]
    hist_sample = jnp.zeros((depth, n_hist, db, conv_dim), F32)
    hist_t = jnp.swapaxes(state_conv, 1, 2)
    for l in range(depth):
        p, *kv_prompt = in_proj_prompt(xp, g_mix3, w_in, l, n_sb_heads, SB_HEAD_DIM, kv_prompt, tm=tm_p, tn=512)
        o_sb = sb_prompt(p, b_sb, l, n_sb_heads, tq=PROMPT_Q_BLOCK)
        c, tail = conv_prompt(p, w_dw, b_dw3, g_cln3, b_cln3, l, tb=256)
        xp = matmul_residual([o_sb, c], w_out, l, xp, tm=2 * tm_p, tn=512)
        cvp.append(tail[tail.shape[0] - n_hist:])

        ps = norm_matmul(xs, g_mix3, w_in, l, tm=ms, tn=tn_s)
        ps3 = ps.reshape(db, n_t, ps.shape[-1])
        o_sb = sb_sample(ps3, cache_k, cache_v, page_table, bias_lanes, l, n_sb_heads)
        c, hist_sample = conv_sample(ps3, hist_t, w_dw, b_dw3, g_cln3, b_cln3, l, hist_sample, bb=8)
        c = jnp.swapaxes(c, 0, 1).reshape(ms, conv_dim)
        xs = matmul_residual([o_sb.reshape(ms, sb_dim), c], w_out, l, xs, tm=ms, tn=tn_s)
        sks.append(ps[:, sb_dim:2 * sb_dim])
        svs.append(ps[:, 2 * sb_dim:3 * sb_dim])

        mk = norm_matmul(mem, g_mem3, w_xk, l, tm=n_mem, tn=tn_s)
        mv = norm_matmul(mem, g_mem3, w_xv, l, tm=n_mem, tn=tn_s)
        mkp.append(mk)
        mvp.append(mv)
        xp = cross_shared_residual(xp, g_cross3, w_xq, mk, mv, w_xo, l, n_x_heads, tm=tm_p)
        qs = norm_matmul(xs, g_cross3, w_xq, l, tm=ms, tn=tn_s)
        o_x = cross_sample(qs.reshape(db, n_t, d), cache_mem_k, cache_mem_v, l, bb=4)
        xs = matmul_residual([o_x.reshape(ms, d)], w_xo, l, xs, tm=ms, tn=tn_s)

        g_last = g_final.reshape(1, d) if l == depth - 1 else None
        xp = ffn_residual(xp, g_ffn3, w_gate, w_up, w_down, l, tm=tm_p, tf=256, g_final2=g_last)
        xs = ffn_residual(xs, g_ffn3, w_gate, w_up, w_down, l, tm=ms, tf=512, g_final2=g_last)

    y_prompt = xp.reshape(batch, seq, d)
    y_sample = xs.reshape(db, n_t, d)
    sb_shape = lambda rows: (depth,) + rows + (n_sb_heads, SB_HEAD_DIM)
    return (
        y_prompt,
        y_sample,
        kv_prompt[0].reshape(sb_shape((batch, seq))),
        kv_prompt[1].reshape(sb_shape((batch, seq))),
        jnp.stack(sks).reshape(sb_shape((db, n_t))),
        jnp.stack(svs).reshape(sb_shape((db, n_t))),
        jnp.stack(cvp).reshape(depth, batch, n_hist, conv_dim),
        jnp.swapaxes(hist_sample, 1, 2),
        jnp.stack(mkp).reshape(depth, batch, n_mem, n_x_heads, x_hd),
        jnp.stack(mvp).reshape(depth, batch, n_mem, n_x_heads, x_hd),
    )
```
